```python
import math
import jax, jax.numpy as jnp
from jax import lax
import numpy as np

D_MODEL = 4096
BATCH = 16
SEQ = 256
DEPTH = 2
DEC_BATCH = 8
DEC_SEQ = 4096
PAST_LEN = 256

GRID_W = 64
SC_W = D_MODEL // 4
NA_HEAD_DIM = 128
NA_HEADS = (D_MODEL // 2) // NA_HEAD_DIM
NA_W = NA_HEADS * NA_HEAD_DIM
NA_KH = 8
NA_KW = 16
HY_W = D_MODEL // 4
HY_EMB = 33
HY_BANDS = (HY_EMB - 1) // 2
HY_FO = 64
HY_FAST = 0.3
HY_SLOW = 1.5
HY_TARGET = 1e-2
N_BRANCH = 3
IN_COLS = 3 * SC_W + 3 * NA_W + 3 * HY_W + N_BRANCH * D_MODEL
FFN_HIDDEN = ((8 * D_MODEL + 3 * 256 - 1) // (3 * 256)) * 256
Q_BLOCK = 128
RMS_EPS = 1e-6

kernel_name = 'hybrid_dit_shortconv_natten_hyena_step'

F32 = jnp.float32


def _rmsnorm(x, g):
    x32 = x.astype(F32)
    y = x32 * lax.rsqrt(jnp.mean(x32 * x32, axis=-1, keepdims=True) + RMS_EPS)
    return (y * g.astype(F32)).astype(x.dtype)


def _dwconv3(x, w):
    xp = jnp.pad(x, ((0, 0), (1, 1), (0, 0)))
    return xp[:, :-2] * w[0] + xp[:, 1:-1] * w[1] + xp[:, 2:] * w[2]


def _context_attention(q, k, v):
    B, L, H, Dh = q.shape
    scale = Dh ** -0.5
    nb = L // Q_BLOCK
    qb = jnp.moveaxis(q.reshape(B, nb, Q_BLOCK, H, Dh), 1, 0)

    def blk(qi):
        s = jnp.einsum('bqhd,bkhd->bhqk', qi, k).astype(F32) * scale
        p = jax.nn.softmax(s, axis=-1).astype(v.dtype)
        return jnp.einsum('bhqk,bkhd->bqhd', p, v)

    o = lax.map(blk, qb)
    return jnp.moveaxis(o, 0, 1).reshape(B, L, H * Dh)


def _neighbourhood_attention(q, k, v, k_ctx, v_ctx, rpb):
    B, L, H, Dh = q.shape
    rows = L // GRID_W
    kh = min(NA_KH, rows)
    ncb = GRID_W // NA_KW
    scale = Dh ** -0.5
    qg = q.reshape(B, rows, ncb, NA_KW, H, Dh)
    kg = k.reshape(B, rows, GRID_W, H, Dh)
    vg = v.reshape(B, rows, GRID_W, H, Dh)
    qcols = np.arange(GRID_W).reshape(ncb, NA_KW)
    cstart = np.clip(qcols - NA_KW // 2, 0, GRID_W - NA_KW)
    bstart = np.clip(np.arange(ncb) * NA_KW - NA_KW // 2, 0, GRID_W - 2 * NA_KW)
    kcols = bstart[:, None] + np.arange(2 * NA_KW)
    kc = kcols[:, None, :]
    col_ok = (kc >= cstart[..., None]) & (kc < cstart[..., None] + NA_KW)
    cidx = np.clip(kc - qcols[..., None], 1 - NA_KW, NA_KW - 1) + NA_KW - 1
    nloc = kh * 2 * NA_KW

    def row_block(r):
        rs = jnp.clip(r - kh // 2, 0, rows - kh)
        q_r = lax.dynamic_index_in_dim(qg, r, axis=1, keepdims=False)
        k_b = lax.dynamic_slice_in_dim(kg, rs, kh, axis=1)[:, :, kcols]
        v_b = lax.dynamic_slice_in_dim(vg, rs, kh, axis=1)[:, :, kcols]
        ridx = rs + jnp.arange(kh) - r + NA_KH - 1
        bias = rpb[:, ridx[None, None, :, None], cidx[:, :, None, :]].astype(F32)
        s_loc = jnp.einsum('bnqhd,bjnkhd->bhnqjk', q_r, k_b).astype(F32) * scale + bias[None]
        s_loc = jnp.where(col_ok[:, :, None, :], s_loc, -jnp.inf).reshape(B, H, ncb, NA_KW, nloc)
        s_ctx = jnp.einsum('bnqhd,bchd->bhnqc', q_r, k_ctx).astype(F32) * scale
        p = jax.nn.softmax(jnp.concatenate([s_loc, s_ctx], axis=-1), axis=-1).astype(v.dtype)
        p_loc = p[..., :nloc].reshape(B, H, ncb, NA_KW, kh, 2 * NA_KW)
        return (jnp.einsum('bhnqjk,bjnkhd->bnqhd', p_loc, v_b)
                + jnp.einsum('bhnqc,bchd->bnqhd', p[..., nloc:], v_ctx))

    o = lax.map(row_block, jnp.arange(rows))
    return jnp.moveaxis(o, 0, 1).reshape(B, L, H * Dh)


def _hyena_filters(L, w1, b1, freq, w2, b2, w3):
    t = jnp.linspace(0.0, 1.0, L, dtype=F32)[:, None]
    ang = (2.0 * math.pi / L) * jnp.arange(L, dtype=F32)[:, None] * jnp.linspace(1e-4, HY_BANDS - 1, HY_BANDS, dtype=F32)[None, :]
    z = jnp.concatenate([t, jnp.cos(ang), -jnp.sin(ang)], axis=-1)
    h = jnp.sin(freq[0].astype(F32) * (z @ w1.astype(F32) + b1.astype(F32)))
    h = jnp.sin(freq[1].astype(F32) * (h @ w2.astype(F32) + b2.astype(F32)))
    h = (h @ w3.astype(F32)).reshape(L, 2, HY_W)
    deltas = jnp.abs(jnp.linspace(math.log(HY_TARGET) / HY_SLOW, math.log(HY_TARGET) / HY_FAST, HY_W, dtype=F32))
    decay = jnp.exp(-t * deltas[None, :])
    return h * decay[:, None, :]


def _bidir_longconv(z, h, bias):
    L = z.shape[1]
    n = 2 * L
    filt = jnp.concatenate([h[:, 0], jnp.zeros((1, h.shape[-1]), h.dtype), h[:0:-1, 1]], axis=0)
    f_f = jnp.fft.rfft(filt, n=n, axis=0)
    z_f = jnp.fft.rfft(z, n=n, axis=1)
    y = jnp.fft.irfft(z_f * f_f[None], n=n, axis=1)[:, :L]
    return y + z * bias


def _hyena(p, conv_w, w1, b1, freq, w2, b2, w3, bias):
    u = _dwconv3(p, conv_w)
    x0, x1, v = jnp.split(u, 3, axis=-1)
    h = _hyena_filters(p.shape[1], w1, b1, freq, w2, b2, w3)
    y = _bidir_longconv((x1 * v).astype(F32), h, bias.astype(F32))
    return x0 * y.astype(p.dtype)


def _mixer(h, ctx_kv, w_in, conv_a, w_up_a, rpb, w_up_b, conv_c, fw1, fb1, ffreq, fw2, fb2, fw3, hbias, w_up_c, w_out):
    B, L, _ = h.shape
    proj = h @ w_in
    s1 = 3 * SC_W
    s2 = s1 + 3 * NA_W
    s3 = s2 + 3 * HY_W
    p_sc, p_na, p_hy, p_g = proj[..., :s1], proj[..., s1:s2], proj[..., s2:s3], proj[..., s3:]
    b_g, c_g, xa = jnp.split(p_sc, 3, axis=-1)
    y_sc = b_g * _dwconv3(c_g * xa, conv_a)
    q, k, v = [t.reshape(B, L, NA_HEADS, NA_HEAD_DIM) for t in jnp.split(p_na, 3, axis=-1)]
    if ctx_kv is None:
        y_na = _context_attention(q, k, v)
    else:
        y_na = _neighbourhood_attention(q, k, v, ctx_kv[0], ctx_kv[1], rpb)
    y_hy = _hyena(p_hy, conv_c, fw1, fb1, ffreq, fw2, fb2, fw3, hbias)
    g = jax.nn.sigmoid(p_g).reshape(B, L, N_BRANCH, D_MODEL)
    merged = g[..., 0, :] * (y_sc @ w_up_a) + g[..., 1, :] * (y_na @ w_up_b) + g[..., 2, :] * (y_hy @ w_up_c)
    return merged @ w_out, k, v


def _swiglu(h, wg, wu, wd):
    return (jax.nn.silu(h @ wg) * (h @ wu)) @ wd


def setup_inputs(seed: int = 0) -> dict:
    key = jax.random.key(seed)
    ks = jax.random.split(key, 32)
    D = D_MODEL

    def nrm(k, shape, scale):
        return jax.random.normal(k, shape, F32) * scale

    return {
        'x_prompt': nrm(ks[0], (BATCH, SEQ, D), 1.0),
        'x_sample': nrm(ks[1], (DEC_BATCH, DEC_SEQ, D), 1.0),
        'cache_k': nrm(ks[2], (DEC_BATCH, DEPTH, PAST_LEN, NA_HEADS, NA_HEAD_DIM), 1.0),
        'cache_v': nrm(ks[3], (DEC_BATCH, DEPTH, PAST_LEN, NA_HEADS, NA_HEAD_DIM), 1.0),
        'c': nrm(ks[4], (DEC_BATCH, D), 1.0),
        'c_ctx': nrm(ks[5], (D,), 1.0),
        'w_mod': nrm(ks[6], (DEPTH, D, 6 * D), 0.5 * D ** -0.5),
        'b_mod': nrm(ks[7], (DEPTH, 6 * D), 0.01),
        'norm_gains': 1.0 + nrm(ks[8], (DEPTH, 4, D), 0.01),
        'w_in': nrm(ks[9], (DEPTH, D, IN_COLS), D ** -0.5),
        'conv_a': nrm(ks[10], (DEPTH, 3, SC_W), 0.5),
        'w_up_a': nrm(ks[11], (DEPTH, SC_W, D), SC_W ** -0.5),
        'na_rpb': nrm(ks[12], (DEPTH, NA_HEADS, 2 * NA_KH - 1, 2 * NA_KW - 1), 0.1),
        'w_up_b': nrm(ks[13], (DEPTH, NA_W, D), NA_W ** -0.5),
        'conv_c': nrm(ks[14], (DEPTH, 3, 3 * HY_W), 0.5),
        'filt_w1': nrm(ks[15], (DEPTH, HY_EMB, HY_FO), HY_EMB ** -0.5),
        'filt_b1': nrm(ks[16], (DEPTH, HY_FO), 0.1),
        'filt_freq': 1.0 + nrm(ks[17], (DEPTH, 2, HY_FO), 0.01),
        'filt_w2': nrm(ks[18], (DEPTH, HY_FO, HY_FO), HY_FO ** -0.5),
        'filt_b2': nrm(ks[19], (DEPTH, HY_FO), 0.1),
        'filt_w3': nrm(ks[20], (DEPTH, HY_FO, 2 * HY_W), 0.02),
        'hyena_bias': nrm(ks[21], (DEPTH, HY_W), 0.1),
        'w_up_c': nrm(ks[22], (DEPTH, HY_W, D), HY_W ** -0.5),
        'w_out': nrm(ks[23], (DEPTH, D, D), D ** -0.5),
        'w_ffn_gate': nrm(ks[24], (DEPTH, D, FFN_HIDDEN), D ** -0.5),
        'w_ffn_up': nrm(ks[25], (DEPTH, D, FFN_HIDDEN), D ** -0.5),
        'w_ffn_down': nrm(ks[26], (DEPTH, FFN_HIDDEN, D), FFN_HIDDEN ** -0.5),
    }


def reference(x_prompt, x_sample, cache_k, cache_v, c, c_ctx, w_mod, b_mod, norm_gains, w_in, conv_a, w_up_a, na_rpb, w_up_b, conv_c, filt_w1, filt_b1, filt_freq, filt_w2, filt_b2, filt_w3, hyena_bias, w_up_c, w_out, w_ffn_gate, w_ffn_up, w_ffn_down):
    def layer(x, cvec, l, ctx_kv):
        mod = (jax.nn.silu(cvec) @ w_mod[l] + b_mod[l])[:, None, :]
        sh1, sc1, g1, sh2, sc2, g2 = jnp.split(mod, 6, axis=-1)
        h = _rmsnorm(x, norm_gains[l, 0]) * (1 + sc1) + sh1
        o, k, v = _mixer(h, ctx_kv, w_in[l], conv_a[l], w_up_a[l], na_rpb[l], w_up_b[l], conv_c[l],
                         filt_w1[l], filt_b1[l], filt_freq[l], filt_w2[l], filt_b2[l], filt_w3[l],
                         hyena_bias[l], w_up_c[l], w_out[l])
        x = x + g1 * _rmsnorm(o, norm_gains[l, 1])
        h = _rmsnorm(x, norm_gains[l, 2]) * (1 + sc2) + sh2
        x = x + g2 * _rmsnorm(_swiglu(h, w_ffn_gate[l], w_ffn_up[l], w_ffn_down[l]), norm_gains[l, 3])
        return x, k, v

    xp = x_prompt
    cctx = c_ctx[None]
    ks, vs = [], []
    for l in range(DEPTH):
        xp, k, v = layer(xp, cctx, l, None)
        ks.append(k)
        vs.append(v)
    new_cache_k = jnp.stack(ks, axis=1)
    new_cache_v = jnp.stack(vs, axis=1)

    xs = x_sample
    for l in range(DEPTH):
        xs, _, _ = layer(xs, c, l, (cache_k[:, l], cache_v[:, l]))

    return (xp, xs, new_cache_k, new_cache_v)
```

```python
import functools
import math

import numpy as np
import jax
import jax.numpy as jnp
from jax import lax
from jax.experimental import pallas as pl
from jax.experimental.pallas import tpu as pltpu

F32 = jnp.float32
BF16 = jnp.bfloat16

GRID_W = 64
HY_FAST = 0.3
HY_SLOW = 1.5
HY_TARGET = 1e-2
RMS_EPS = 1e-6
N_MOD = 6
N_BRANCH = 3

V7X_VMEM_BYTES = 64 * 1024 * 1024
V7X_VMEM_RESERVED_BYTES = 6 * 1024 * 1024
V7X_LANES = 128
V7X_SUBLANES = 8
MOD_ROWS_PAD = 16

TM_MM = 1024
TN_MM = 1024
TN_HALF = 512
TM_EPI = 512
TK_EPI = 512
EPI_COL_CHUNK = 1024
EPI_ROW_CHUNK = 64
TM_DFT = 512
TN_DFT = 512
TC_CONV = 128
FFN_PAD = 1024
HEADS_PER_STEP_CTX = 8
DFT_SPLIT = 64


def _tile(dim, pref):
    return pref if dim % pref == 0 else dim


def _nbytes(shape, dtype):
    return int(np.prod(shape)) * jnp.dtype(dtype).itemsize


def _params(semantics, block_bytes, temp_bytes=0):
    need = 2 * block_bytes + temp_bytes
    limit = min(V7X_VMEM_BYTES - V7X_VMEM_RESERVED_BYTES, max(need, 16 * 1024 * 1024))
    return pltpu.CompilerParams(dimension_semantics=semantics, vmem_limit_bytes=limit)


def _dot(a, b):
    return jnp.dot(a, b, preferred_element_type=F32)


def _dot_t(a, b):
    return lax.dot_general(a, b, (((1,), (1,)), ((), ())), preferred_element_type=F32)


def _rms(x, gain):
    return x * lax.rsqrt(jnp.mean(x * x, axis=-1, keepdims=True) + RMS_EPS) * gain


def _mod_kernel(c_ref, w_ref, b_ref, o_ref):
    a = jax.nn.silu(c_ref[...]).astype(BF16)
    o_ref[...] = _dot(a, w_ref[...].astype(BF16)) + b_ref[...]


def _modulation(c_rows, w_mod, b_mod):
    depth, d, n = w_mod.shape
    tn = _tile(n, TN_HALF)
    blocks = _nbytes((MOD_ROWS_PAD, d), F32) + _nbytes((d, tn), F32) + _nbytes((MOD_ROWS_PAD + 1, tn), F32)
    return pl.pallas_call(
        _mod_kernel,
        out_shape=jax.ShapeDtypeStruct((depth, MOD_ROWS_PAD, n), F32),
        grid=(depth, n // tn),
        in_specs=[pl.BlockSpec((MOD_ROWS_PAD, d), lambda l, j: (0, 0)),
                  pl.BlockSpec((None, d, tn), lambda l, j: (l, 0, j)),
                  pl.BlockSpec((None, 1, tn), lambda l, j: (l, 0, j))],
        out_specs=pl.BlockSpec((None, MOD_ROWS_PAD, tn), lambda l, j: (l, 0, j)),
        compiler_params=_params(("arbitrary", "arbitrary"), blocks, _nbytes((d, tn), BF16)),
        name="modulation",
    )(c_rows, w_mod, b_mod.reshape(depth, 1, n))


def _row_tiling(mod_l, total_rows, rows_per_batch, pref):
    shared = mod_l.shape[0] == 1
    tm = _tile(total_rows if shared else rows_per_batch, pref)
    return tm, (total_rows if shared else rows_per_batch) // tm


def _vec_spec(d, which, tiles_per_batch, grid_rank):
    if grid_rank == 1:
        return pl.BlockSpec((None, None, 1, d), lambda i: (i // tiles_per_batch, which, 0, 0))
    return pl.BlockSpec((None, None, 1, d), lambda i, k: (i // tiles_per_batch, which, 0, 0))


def _row_spec(d, grid_rank):
    if grid_rank == 1:
        return pl.BlockSpec((1, d), lambda i: (0, 0))
    return pl.BlockSpec((1, d), lambda i, k: (0, 0))


def _prenorm_kernel(x_ref, g_ref, sc_ref, sh_ref, h_ref):
    y = _rms(x_ref[...], g_ref[...])
    h_ref[...] = (y * (1.0 + sc_ref[...]) + sh_ref[...]).astype(BF16)


def _prenorm(x2, gain, mod_l, rows_per_batch):
    t, d = x2.shape
    tm, tpb = _row_tiling(mod_l, t, rows_per_batch, TM_EPI)
    blocks = _nbytes((tm, d), F32) + _nbytes((tm, d), BF16) + 3 * _nbytes((1, d), F32)
    return pl.pallas_call(
        _prenorm_kernel,
        out_shape=jax.ShapeDtypeStruct((t, d), BF16),
        grid=(t // tm,),
        in_specs=[pl.BlockSpec((tm, d), lambda i: (i, 0)),
                  _row_spec(d, 1),
                  _vec_spec(d, 1, tpb, 1),
                  _vec_spec(d, 0, tpb, 1)],
        out_specs=pl.BlockSpec((tm, d), lambda i: (i, 0)),
        compiler_params=_params(("arbitrary",), blocks, 2 * _nbytes((tm, d), F32)),
        name="prenorm",
    )(x2, gain.reshape(1, d), mod_l, mod_l)


def _mm_kernel(a_ref, b_ref, o_ref):
    o_ref[...] = _dot(a_ref[...], b_ref[...]).astype(o_ref.dtype)


def _matmul(a, b, out_dtype):
    m, k = a.shape
    n = b.shape[1]
    tm, tn = _tile(m, TM_MM), _tile(n, TN_MM)
    blocks = _nbytes((tm, k), BF16) + _nbytes((k, tn), BF16) + _nbytes((tm, tn), out_dtype)
    return pl.pallas_call(
        _mm_kernel,
        out_shape=jax.ShapeDtypeStruct((m, n), out_dtype),
        grid=(m // tm, n // tn),
        in_specs=[pl.BlockSpec((tm, k), lambda i, j: (i, 0)),
                  pl.BlockSpec((k, tn), lambda i, j: (0, j))],
        out_specs=pl.BlockSpec((tm, tn), lambda i, j: (i, j)),
        compiler_params=_params(("arbitrary", "arbitrary"), blocks, 3 * _nbytes((tm, tn), F32)),
        name="in_proj",
    )(a, b)


def _dwconv3(s, w):
    n = s.shape[0]
    row = lax.broadcasted_iota(jnp.int32, s.shape, 0)
    prev = jnp.where(row == 0, 0.0, pltpu.roll(s, 1, 0))
    nxt = jnp.where(row == n - 1, 0.0, pltpu.roll(s, n - 1, 0))
    return prev * w[0:1, :] + s * w[1:2, :] + nxt * w[2:3, :]


def _sconv_kernel(b_ref, c_ref, x_ref, w_ref, o_ref):
    s = c_ref[...].astype(F32) * x_ref[...].astype(F32)
    o_ref[...] = (b_ref[...].astype(F32) * _dwconv3(s, w_ref[...])).astype(BF16)


def _short_conv(proj3, conv_w, col0):
    bsz, length, _ = proj3.shape
    width = conv_w.shape[1]
    tc = _tile(width, TC_CONV)
    nb = width // tc
    base = col0 // tc
    blocks = 3 * _nbytes((length, tc), proj3.dtype) + _nbytes((3, tc), F32) + _nbytes((length, tc), BF16)

    def col(g):
        return pl.BlockSpec((None, length, tc), lambda b, j: (b, 0, base + g * nb + j))

    return pl.pallas_call(
        _sconv_kernel,
        out_shape=jax.ShapeDtypeStruct((bsz, length, width), BF16),
        grid=(bsz, nb),
        in_specs=[col(0), col(1), col(2), pl.BlockSpec((3, tc), lambda b, j: (0, j))],
        out_specs=pl.BlockSpec((None, length, tc), lambda b, j: (b, 0, j)),
        compiler_params=_params(("arbitrary", "arbitrary"), blocks, 6 * _nbytes((length, tc), F32)),
        name="short_conv",
    )(proj3, proj3, proj3, conv_w)


def _hyena_pre_kernel(p0_ref, p1_ref, p2_ref, w0_ref, w1_ref, w2_ref, x0_ref, z_ref):
    x0_ref[...] = _dwconv3(p0_ref[...].astype(F32), w0_ref[...]).astype(BF16)
    x1 = _dwconv3(p1_ref[...].astype(F32), w1_ref[...])
    v = _dwconv3(p2_ref[...].astype(F32), w2_ref[...])
    z_ref[...] = (x1 * v).astype(BF16)


def _hyena_pre(proj3, conv_w, col0):
    bsz, length, _ = proj3.shape
    width = conv_w.shape[1] // 3
    tc = _tile(width, TC_CONV)
    nb = width // tc
    base = col0 // tc
    blocks = (3 * _nbytes((length, tc), proj3.dtype) + 3 * _nbytes((3, tc), F32)
              + 2 * _nbytes((length, tc), BF16))

    def col(g):
        return pl.BlockSpec((None, length, tc), lambda b, j: (b, 0, base + g * nb + j))

    def wcol(g):
        return pl.BlockSpec((3, tc), lambda b, j: (0, g * nb + j))

    out = jax.ShapeDtypeStruct((bsz, length, width), BF16)
    ospec = pl.BlockSpec((None, length, tc), lambda b, j: (b, 0, j))
    return pl.pallas_call(
        _hyena_pre_kernel,
        out_shape=(out, out),
        grid=(bsz, nb),
        in_specs=[col(0), col(1), col(2), wcol(0), wcol(1), wcol(2)],
        out_specs=(ospec, ospec),
        compiler_params=_params(("arbitrary", "arbitrary"), blocks, 8 * _nbytes((length, tc), F32)),
        name="hyena_pre",
    )(proj3, proj3, proj3, conv_w, conv_w, conv_w)


def _softmax_pv(parts):
    m = functools.reduce(jnp.maximum, [jnp.max(s, axis=-1, keepdims=True) for s, _ in parts])
    ps = [jnp.exp(s - m) for s, _ in parts]
    denom = functools.reduce(jnp.add, [jnp.sum(p, axis=-1, keepdims=True) for p in ps])
    acc = functools.reduce(jnp.add, [_dot(p.astype(BF16), v) for p, (_, v) in zip(ps, parts)])
    return acc / denom


def _ctx_attn_kernel(q_ref, k_ref, v_ref, o_ref, *, heads, head_dim):
    scale = head_dim ** -0.5
    for h in range(heads):
        sl = slice(h * head_dim, (h + 1) * head_dim)
        q = q_ref[:, sl].astype(BF16)
        k = k_ref[:, sl].astype(BF16)
        v = v_ref[:, sl].astype(BF16)
        o_ref[:, sl] = _softmax_pv([(_dot_t(q, k) * scale, v)]).astype(BF16)


def _context_attention(proj3, col0, n_heads, head_dim):
    bsz, length, _ = proj3.shape
    hp = max(h for h in range(1, HEADS_PER_STEP_CTX + 1)
             if n_heads % h == 0 and col0 % (h * head_dim) == 0)
    bw = hp * head_dim
    nb = n_heads // hp
    base = col0 // bw
    blocks = 3 * _nbytes((length, bw), proj3.dtype) + _nbytes((length, bw), BF16)

    def col(g):
        return pl.BlockSpec((None, length, bw), lambda b, j: (b, 0, base + g * nb + j))

    return pl.pallas_call(
        functools.partial(_ctx_attn_kernel, heads=hp, head_dim=head_dim),
        out_shape=jax.ShapeDtypeStruct((bsz, length, n_heads * head_dim), BF16),
        grid=(bsz, nb),
        in_specs=[col(0), col(1), col(2)],
        out_specs=pl.BlockSpec((None, length, bw), lambda b, j: (b, 0, j)),
        compiler_params=_params(("arbitrary", "arbitrary"), blocks, 8 * _nbytes((length, length), F32)),
        name="context_attention",
    )(proj3, proj3, proj3)


def _nat_kernel(q_ref, k_ref, v_ref, kc_ref, vc_ref, bias_ref, o_ref, *, rows, kh, head_dim):
    scale = head_dim ** -0.5
    kc = kc_ref[...].astype(BF16)
    vc = vc_ref[...].astype(BF16)

    def body(r, carry):
        rs = jnp.clip(r - kh // 2, 0, rows - kh)
        q = q_ref[pl.ds(pl.multiple_of(r * GRID_W, GRID_W), GRID_W), :]
        win = pl.ds(pl.multiple_of(rs * GRID_W, GRID_W), kh * GRID_W)
        s_loc = _dot_t(q, k_ref[win, :]) * scale + bias_ref[rs - r + kh - 1]
        s_ctx = _dot_t(q, kc) * scale
        out = _softmax_pv([(s_loc, v_ref[win, :]), (s_ctx, vc)])
        o_ref[pl.ds(pl.multiple_of(r * GRID_W, GRID_W), GRID_W), :] = out.astype(BF16)
        return carry

    lax.fori_loop(0, rows, body, 0)


def _nat_bias_strips(rpb, kh_full, kw):
    qc = np.arange(GRID_W)[:, None]
    kc = np.arange(GRID_W)[None, :]
    cstart = np.clip(qc - kw // 2, 0, GRID_W - kw)
    ok = (kc >= cstart) & (kc < cstart + kw)
    cidx = np.clip(kc - qc, 1 - kw, kw - 1) + kw - 1
    o = np.arange(kh_full)[:, None] + np.arange(kh_full)[None, :]
    t = rpb[:, o[:, :, None, None], cidx[None, None, :, :]]
    t = jnp.where(ok[None, None, None], t, -jnp.inf)
    n_heads = rpb.shape[0]
    return jnp.transpose(t, (0, 1, 3, 2, 4)).reshape(n_heads, kh_full, GRID_W, kh_full * GRID_W)


def _neighbourhood_attention(proj3, col0, cache_k4, cache_v4, layer, strips, n_heads, head_dim):
    bsz, length, _ = proj3.shape
    rows = length // GRID_W
    kh = strips.shape[1]
    past = cache_k4.shape[2]
    base = col0 // head_dim
    blocks = (3 * _nbytes((length, head_dim), proj3.dtype) + 2 * _nbytes((past, head_dim), F32)
              + _nbytes(strips.shape[1:], F32) + _nbytes((length, head_dim), BF16))

    def col(g):
        return pl.BlockSpec((None, length, head_dim), lambda b, h: (b, 0, base + g * n_heads + h))

    cache_spec = pl.BlockSpec((None, None, past, head_dim), lambda b, h: (b, layer, 0, h))
    return pl.pallas_call(
        functools.partial(_nat_kernel, rows=rows, kh=kh, head_dim=head_dim),
        out_shape=jax.ShapeDtypeStruct((bsz, length, n_heads * head_dim), BF16),
        grid=(bsz, n_heads),
        in_specs=[col(0), col(1), col(2), cache_spec, cache_spec,
                  pl.BlockSpec((None,) + strips.shape[1:], lambda b, h: (h, 0, 0, 0))],
        out_specs=pl.BlockSpec((None, length, head_dim), lambda b, h: (b, 0, h)),
        compiler_params=_params(("arbitrary", "arbitrary"), blocks,
                                8 * _nbytes((GRID_W, kh * GRID_W + past), F32)),
        name="neighbourhood_attention",
    )(proj3, proj3, proj3, cache_k4, cache_v4, strips)


def _dft_matrices(length):
    n = 2 * length
    split = min(DFT_SPLIT, length)
    s = np.arange(length, dtype=np.int64)
    pa = ((np.arange(length // split, dtype=np.int64)[:, None] * split * s[None, :]) % n).astype(np.int32)
    pb = ((np.arange(split, dtype=np.int64)[:, None] * s[None, :]) % n).astype(np.int32)
    ta = jnp.asarray(pa).astype(F32) * (2.0 * math.pi / n)
    tb = jnp.asarray(pb).astype(F32) * (2.0 * math.pi / n)
    ca, sa, cb, sb = jnp.cos(ta), jnp.sin(ta), jnp.cos(tb), jnp.sin(tb)
    cosm = (ca[:, None, :] * cb[None, :, :] - sa[:, None, :] * sb[None, :, :]).reshape(length, length)
    msin = -(sa[:, None, :] * cb[None, :, :] + ca[:, None, :] * sb[None, :, :]).reshape(length, length)
    k_idx = lax.broadcasted_iota(jnp.int32, (length, length), 0)
    s_idx = lax.broadcasted_iota(jnp.int32, (length, length), 1)
    sinm = jnp.where(k_idx == 0, jnp.where(s_idx % 2 == 0, 1.0, -1.0), msin)
    sinm_t = jnp.where(s_idx == 0, jnp.where(k_idx % 2 == 0, 1.0, -1.0), msin)
    return cosm.astype(BF16), sinm.astype(BF16), sinm_t.astype(BF16)


def _filter_kernel(bands_ref, w1_ref, b1_ref, f_ref, w2_ref, b2_ref, w3_ref, dl_ref, o_ref,
                   *, length, tl, emb_bands, width):
    hi = lax.Precision.HIGHEST
    pos = (lax.broadcasted_iota(jnp.int32, (tl, V7X_LANES), 0) + pl.program_id(0) * tl).astype(F32)
    lane = lax.broadcasted_iota(jnp.int32, (tl, V7X_LANES), 1)
    t = pos * (1.0 / (length - 1))
    ang = ((2.0 * math.pi / length) * pos) * bands_ref[...]
    feat = jnp.where(lane == 0, t,
                     jnp.where(lane <= emb_bands, jnp.cos(ang),
                               jnp.where(lane <= 2 * emb_bands, -jnp.sin(ang), 0.0)))
    h = jnp.sin(f_ref[0:1, :] * (jnp.dot(feat, w1_ref[...], precision=hi, preferred_element_type=F32)
                                 + b1_ref[...]))
    h = jnp.sin(f_ref[1:2, :] * (jnp.dot(h, w2_ref[...], precision=hi, preferred_element_type=F32)
                                 + b2_ref[...]))
    h = jnp.dot(h, w3_ref[...], precision=hi, preferred_element_type=F32)
    decay = jnp.exp(-(t[:, 0:1]) * dl_ref[...])
    col = lax.broadcasted_iota(jnp.int32, h.shape, 1)
    first = (pos[:, 0:1] == 0.0) & (col >= width)
    o_ref[...] = jnp.where(first, 0.0, h * decay).astype(BF16)


def _hyena_filters(length, w1, b1, freq, w2, b2, w3):
    emb, fo = w1.shape
    width = w3.shape[1] // 2
    emb_bands = (emb - 1) // 2
    tl = _tile(length, TM_DFT)
    bands = jnp.linspace(1e-4, emb_bands - 1, emb_bands, dtype=F32)
    bands_row = jnp.zeros((1, V7X_LANES), F32).at[0, 1:1 + emb_bands].set(bands)
    bands_row = bands_row.at[0, 1 + emb_bands:1 + 2 * emb_bands].set(bands)
    w1p = jnp.zeros((V7X_LANES, fo), F32).at[:emb].set(w1)
    deltas = jnp.abs(jnp.linspace(math.log(HY_TARGET) / HY_SLOW, math.log(HY_TARGET) / HY_FAST, width, dtype=F32))
    dl = jnp.concatenate([deltas, deltas]).reshape(1, 2 * width)
    full = lambda shape: pl.BlockSpec(shape, lambda i: (0,) * len(shape))
    blocks = _nbytes((tl, 2 * width), BF16) + _nbytes((fo + 2, 2 * width), F32) + _nbytes((V7X_LANES + fo, fo), F32)
    return pl.pallas_call(
        functools.partial(_filter_kernel, length=length, tl=tl, emb_bands=emb_bands, width=width),
        out_shape=jax.ShapeDtypeStruct((length, 2 * width), BF16),
        grid=(length // tl,),
        in_specs=[full((1, V7X_LANES)), full((V7X_LANES, fo)), full((1, fo)), full((2, fo)),
                  full((fo, fo)), full((1, fo)), full((fo, 2 * width)), full((1, 2 * width))],
        out_specs=pl.BlockSpec((tl, 2 * width), lambda i: (i, 0)),
        compiler_params=_params(("arbitrary",), blocks, 6 * _nbytes((tl, 2 * width), F32)),
        name="hyena_filters",
    )(bands_row, w1p, b1.reshape(1, fo), freq, w2, b2.reshape(1, fo), w3, dl)


def _spectrum_kernel(c_ref, s_ref, hf_ref, hb_ref, fr_ref, fi_ref, *, tm, norm):
    cm, sm, hf, hb = c_ref[...], s_ref[...], hf_ref[...], hb_ref[...]
    row = lax.broadcasted_iota(jnp.int32, fr_ref.shape, 0) + pl.program_id(0) * tm
    packed = row == 0
    wgt = jnp.where(packed, 0.5 * norm, norm)
    bi = _dot(sm, hb)
    fr_ref[...] = (_dot(cm, hf) + _dot(cm, hb)) * wgt
    fi_ref[...] = (_dot(sm, hf) + jnp.where(packed, bi, -bi)) * wgt


def _filter_spectrum(cosm, sinm, hcat):
    length = cosm.shape[0]
    width = hcat.shape[1] // 2
    tm, tn = _tile(length, TM_DFT), _tile(width, TN_DFT)
    nb = width // tn
    blocks = 2 * _nbytes((tm, length), BF16) + 2 * _nbytes((length, tn), BF16) + 2 * _nbytes((tm, tn), F32)
    out = jax.ShapeDtypeStruct((length, width), F32)
    ospec = pl.BlockSpec((tm, tn), lambda i, j: (i, j))
    return pl.pallas_call(
        functools.partial(_spectrum_kernel, tm=tm, norm=1.0 / length),
        out_shape=(out, out),
        grid=(length // tm, nb),
        in_specs=[pl.BlockSpec((tm, length), lambda i, j: (i, 0)),
                  pl.BlockSpec((tm, length), lambda i, j: (i, 0)),
                  pl.BlockSpec((length, tn), lambda i, j: (0, j)),
                  pl.BlockSpec((length, tn), lambda i, j: (0, nb + j))],
        out_specs=(ospec, ospec),
        compiler_params=_params(("arbitrary", "arbitrary"), blocks, 16 * _nbytes((tm, tn), F32)),
        name="filter_spectrum",
    )(cosm, sinm, hcat, hcat)


def _dft_fwd_kernel(c_ref, s_ref, z_ref, fr_ref, fi_ref, yr_ref, yi_ref, *, tm):
    z = z_ref[...]
    zr, zi = _dot(c_ref[...], z), _dot(s_ref[...], z)
    fr, fi = fr_ref[...], fi_ref[...]
    packed = (lax.broadcasted_iota(jnp.int32, zr.shape, 0) + pl.program_id(0) * tm) == 0
    yr_ref[...] = jnp.where(packed, zr * fr, zr * fr - zi * fi).astype(BF16)
    yi_ref[...] = jnp.where(packed, zi * fi, zr * fi + zi * fr).astype(BF16)


def _dft_forward(cosm, sinm, z, fr, fi):
    bsz, length, width = z.shape
    tm, tn = _tile(length, TM_DFT), _tile(width, TN_DFT)
    blocks = (2 * _nbytes((tm, length), BF16) + _nbytes((length, tn), BF16) + 2 * _nbytes((tm, tn), F32)
              + 2 * _nbytes((tm, tn), BF16))
    out = jax.ShapeDtypeStruct((bsz, length, width), BF16)
    ospec = pl.BlockSpec((None, tm, tn), lambda i, b, j: (b, i, j))
    return pl.pallas_call(
        functools.partial(_dft_fwd_kernel, tm=tm),
        out_shape=(out, out),
        grid=(length // tm, bsz, width // tn),
        in_specs=[pl.BlockSpec((tm, length), lambda i, b, j: (i, 0)),
                  pl.BlockSpec((tm, length), lambda i, b, j: (i, 0)),
                  pl.BlockSpec((None, length, tn), lambda i, b, j: (b, 0, j)),
                  pl.BlockSpec((tm, tn), lambda i, b, j: (i, j)),
                  pl.BlockSpec((tm, tn), lambda i, b, j: (i, j))],
        out_specs=(ospec, ospec),
        compiler_params=_params(("arbitrary",) * 3, blocks, 16 * _nbytes((tm, tn), F32)),
        name="dft_forward",
    )(cosm, sinm, z, fr, fi)


def _dft_inv_kernel(c_ref, st_ref, yr_ref, yi_ref, x0_ref, z_ref, b_ref, o_ref):
    y = _dot(c_ref[...], yr_ref[...]) + _dot(st_ref[...], yi_ref[...])
    y = y + z_ref[...].astype(F32) * b_ref[...]
    o_ref[...] = (x0_ref[...].astype(F32) * y).astype(BF16)


def _dft_inverse(cosm, sinm_t, yr, yi, x0, z, bias):
    bsz, length, width = z.shape
    tm, tn = _tile(length, TM_DFT), _tile(width, TN_DFT)
    blocks = (2 * _nbytes((tm, length), BF16) + 2 * _nbytes((length, tn), BF16)
              + 3 * _nbytes((tm, tn), BF16) + _nbytes((1, tn), F32))
    tile = pl.BlockSpec((None, tm, tn), lambda i, b, j: (b, i, j))
    panel = pl.BlockSpec((None, length, tn), lambda i, b, j: (b, 0, j))
    return pl.pallas_call(
        _dft_inv_kernel,
        out_shape=jax.ShapeDtypeStruct((bsz, length, width), BF16),
        grid=(length // tm, bsz, width // tn),
        in_specs=[pl.BlockSpec((tm, length), lambda i, b, j: (i, 0)),
                  pl.BlockSpec((tm, length), lambda i, b, j: (i, 0)),
                  panel, panel, tile, tile,
                  pl.BlockSpec((1, tn), lambda i, b, j: (0, j))],
        out_specs=tile,
        compiler_params=_params(("arbitrary",) * 3, blocks, 16 * _nbytes((tm, tn), F32)),
        name="dft_inverse",
    )(cosm, sinm_t, yr, yi, x0, z, bias.reshape(1, width))


def _merge_kernel(ya_ref, yb_ref, yc_ref, ga_ref, gb_ref, gc_ref, wa_ref, wb_ref, wc_ref, o_ref):
    m = jax.nn.sigmoid(ga_ref[...].astype(F32)) * _dot(ya_ref[...], wa_ref[...])
    m = m + jax.nn.sigmoid(gb_ref[...].astype(F32)) * _dot(yb_ref[...], wb_ref[...])
    m = m + jax.nn.sigmoid(gc_ref[...].astype(F32)) * _dot(yc_ref[...], wc_ref[...])
    o_ref[...] = m.astype(BF16)


def _merge(ya, yb, yc, proj, gate_col0, wa, wb, wc):
    t = ya.shape[0]
    d = wa.shape[1]
    tm, tn = _tile(t, TM_MM), _tile(d, TN_HALF)
    nb = d // tn
    base = gate_col0 // tn
    ka, kb, kc = ya.shape[1], yb.shape[1], yc.shape[1]
    blocks = (_nbytes((tm, ka + kb + kc), BF16) + 3 * _nbytes((tm, tn), proj.dtype)
              + _nbytes((ka + kb + kc, tn), BF16) + _nbytes((tm, tn), BF16))

    def gate(g):
        return pl.BlockSpec((tm, tn), lambda i, j: (i, base + g * nb + j))

    def panel(k):
        return pl.BlockSpec((tm, k), lambda i, j: (i, 0))

    def wcol(k):
        return pl.BlockSpec((k, tn), lambda i, j: (0, j))

    return pl.pallas_call(
        _merge_kernel,
        out_shape=jax.ShapeDtypeStruct((t, d), BF16),
        grid=(t // tm, nb),
        in_specs=[panel(ka), panel(kb), panel(kc), gate(0), gate(1), gate(2), wcol(ka), wcol(kb), wcol(kc)],
        out_specs=pl.BlockSpec((tm, tn), lambda i, j: (i, j)),
        compiler_params=_params(("arbitrary", "arbitrary"), blocks, 16 * _nbytes((tm, tn), F32)),
        name="gated_merge",
    )(ya, yb, yc, proj, proj, proj, wa, wb, wc)


def _mm_epilogue_kernel(a_ref, b_ref, x_ref, gpost_ref, gate_ref, *rest, nk, nx, with_next):
    if with_next:
        gnext_ref, sc_ref, sh_ref, xo_ref, h_ref, xs_ref = rest
    else:
        xo_ref, xs_ref = rest
    k = pl.program_id(1)
    tm, d = xo_ref.shape
    cw = _tile(d, EPI_COL_CHUNK)

    @pl.when(k == 0)
    def _():
        xo_ref[...] = jnp.zeros_like(xo_ref)

    @pl.when(k < nx)
    def _():
        xs_ref[k] = x_ref[...]

    a = a_ref[...]
    for c in range(d // cw):
        xo_ref[:, c * cw:(c + 1) * cw] += _dot(a, b_ref[:, c * cw:(c + 1) * cw])

    @pl.when(k == nk - 1)
    def _():
        rc = _tile(tm, EPI_ROW_CHUNK)

        def chunk(ci, carry):
            rows = pl.ds(pl.multiple_of(ci * rc, rc), rc)
            x_in = jnp.concatenate([xs_ref[j, rows, :] for j in range(nx)], axis=-1)
            xn = x_in + gate_ref[...] * _rms(xo_ref[rows, :], gpost_ref[...])
            xo_ref[rows, :] = xn
            if with_next:
                hn = _rms(xn, gnext_ref[...]) * (1.0 + sc_ref[...]) + sh_ref[...]
                h_ref[rows, :] = hn.astype(BF16)
            return carry

        lax.fori_loop(0, tm // rc, chunk, 0)


def _residual_slices(d, nk):
    return max(n for n in range(1, nk + 1) if d % n == 0 and (d // n) % V7X_LANES == 0)


def _matmul_epilogue(a, b, x2, gain_post, mod_l, gate_idx, rows_per_batch, nxt):
    t, kdim = a.shape
    d = b.shape[1]
    tm, tpb = _row_tiling(mod_l, t, rows_per_batch, TM_EPI)
    tk = _tile(kdim, TK_EPI)
    nk = kdim // tk
    nx = _residual_slices(d, nk)
    xw = d // nx
    blocks = (_nbytes((tm, tk), BF16) + _nbytes((tk, d), BF16) + _nbytes((tm, xw), F32)
              + _nbytes((tm, d), F32) + 5 * _nbytes((1, d), F32))
    in_specs = [pl.BlockSpec((tm, tk), lambda i, k: (i, k)),
                pl.BlockSpec((tk, d), lambda i, k: (k, 0)),
                pl.BlockSpec((tm, xw), lambda i, k: (i, jnp.minimum(k, nx - 1))),
                _row_spec(d, 2),
                _vec_spec(d, gate_idx, tpb, 2)]
    args = [a, b, x2, gain_post.reshape(1, d), mod_l]
    row_out = pl.BlockSpec((tm, d), lambda i, k: (i, 0))
    if nxt is None:
        out_shape = jax.ShapeDtypeStruct((t, d), F32)
        out_specs = row_out
    else:
        gain_next, mod_next, sc_idx, sh_idx = nxt
        in_specs += [_row_spec(d, 2), _vec_spec(d, sc_idx, tpb, 2), _vec_spec(d, sh_idx, tpb, 2)]
        args += [gain_next.reshape(1, d), mod_next, mod_next]
        out_shape = (jax.ShapeDtypeStruct((t, d), F32), jax.ShapeDtypeStruct((t, d), BF16))
        out_specs = (row_out, row_out)
        blocks += _nbytes((tm, d), BF16)
    temps = (_nbytes((nx, tm, xw), F32) + 2 * _nbytes((tm, _tile(d, EPI_COL_CHUNK)), F32)
             + 6 * _nbytes((_tile(tm, EPI_ROW_CHUNK), d), F32))
    return pl.pallas_call(
        functools.partial(_mm_epilogue_kernel, nk=nk, nx=nx, with_next=nxt is not None),
        out_shape=out_shape,
        grid=(t // tm, nk),
        in_specs=in_specs,
        out_specs=out_specs,
        scratch_shapes=[pltpu.VMEM((nx, tm, xw), F32)],
        compiler_params=_params(("arbitrary", "arbitrary"), blocks, temps),
        name="matmul_norm_residual",
    )(*args)


def _ffn_up_kernel(h_ref, wg_ref, wu_ref, o_ref):
    h = h_ref[...]
    o_ref[...] = (jax.nn.silu(_dot(h, wg_ref[...])) * _dot(h, wu_ref[...])).astype(BF16)


def _ffn_up(h, wg, wu):
    t, d = h.shape
    n = wg.shape[1]
    tm, tn = _tile(t, TM_MM), _tile(n, TN_HALF)
    blocks = _nbytes((tm, d), BF16) + 2 * _nbytes((d, tn), BF16) + _nbytes((tm, tn), BF16)
    return pl.pallas_call(
        _ffn_up_kernel,
        out_shape=jax.ShapeDtypeStruct((t, n), BF16),
        grid=(t // tm, n // tn),
        in_specs=[pl.BlockSpec((tm, d), lambda i, j: (i, 0)),
                  pl.BlockSpec((d, tn), lambda i, j: (0, j)),
                  pl.BlockSpec((d, tn), lambda i, j: (0, j))],
        out_specs=pl.BlockSpec((tm, tn), lambda i, j: (i, j)),
        compiler_params=_params(("arbitrary", "arbitrary"), blocks, 16 * _nbytes((tm, tn), F32)),
        name="ffn_up",
    )(h, wg, wu)


def _pad_cols(w, mult):
    pad = (-w.shape[-1]) % mult
    return jnp.pad(w, ((0, 0), (0, 0), (0, pad))) if pad else w


def _pad_rows(w, mult):
    pad = (-w.shape[-2]) % mult
    return jnp.pad(w, ((0, 0), (0, pad), (0, 0))) if pad else w


def kernel(x_prompt, x_sample, cache_k, cache_v, c, c_ctx, w_mod, b_mod, norm_gains, w_in, conv_a, w_up_a, na_rpb, w_up_b, conv_c, filt_w1, filt_b1, filt_freq, filt_w2, filt_b2, filt_w3, hyena_bias, w_up_c, w_out, w_ffn_gate, w_ffn_up, w_ffn_down):
    depth, d, _ = w_mod.shape
    n_heads, head_dim = cache_k.shape[3], cache_k.shape[4]
    sc_w = conv_a.shape[-1]
    na_w = n_heads * head_dim
    hy_w = hyena_bias.shape[-1]
    kh_full = (na_rpb.shape[2] + 1) // 2
    kw = (na_rpb.shape[3] + 1) // 2
    col_na = 3 * sc_w
    col_hy = col_na + 3 * na_w
    col_gate = col_hy + 3 * hy_w
    assert x_sample.shape[1] % GRID_W == 0 and x_sample.shape[1] // GRID_W >= kh_full
    assert 1 + c.shape[0] <= MOD_ROWS_PAD

    w_in_b = w_in.astype(BF16)
    w_up_a_b, w_up_b_b, w_up_c_b = w_up_a.astype(BF16), w_up_b.astype(BF16), w_up_c.astype(BF16)
    w_out_b = w_out.astype(BF16)
    w_gate_b = _pad_cols(w_ffn_gate.astype(BF16), FFN_PAD)
    w_upf_b = _pad_cols(w_ffn_up.astype(BF16), FFN_PAD)
    w_down_b = _pad_rows(w_ffn_down.astype(BF16), FFN_PAD)

    c_rows = jnp.zeros((MOD_ROWS_PAD, d), F32).at[0].set(c_ctx).at[1:1 + c.shape[0]].set(c)
    mod = _modulation(c_rows, w_mod, b_mod).reshape(depth, MOD_ROWS_PAD, N_MOD, 1, d)
    cache_k4 = cache_k.reshape(cache_k.shape[:3] + (na_w,))
    cache_v4 = cache_v.reshape(cache_v.shape[:3] + (na_w,))

    def run_group(x3, mod_g, latent):
        bsz, length, _ = x3.shape
        proj_dtype = BF16 if latent else F32
        cosm, sinm, sinm_t = _dft_matrices(length)
        x2 = x3.reshape(bsz * length, d)
        h = _prenorm(x2, norm_gains[0, 0], mod_g[0], length)
        ks, vs = [], []
        for l in range(depth):
            proj = _matmul(h, w_in_b[l], proj_dtype)
            proj3 = proj.reshape(bsz, length, proj.shape[1])
            y_sc = _short_conv(proj3, conv_a[l], 0)
            if latent:
                strips = _nat_bias_strips(na_rpb[l], kh_full, kw)
                y_na = _neighbourhood_attention(proj3, col_na, cache_k4, cache_v4, l, strips, n_heads, head_dim)
            else:
                y_na = _context_attention(proj3, col_na, n_heads, head_dim)
                ks.append(proj3[:, :, col_na + na_w:col_na + 2 * na_w].reshape(bsz, length, n_heads, head_dim))
                vs.append(proj3[:, :, col_na + 2 * na_w:col_hy].reshape(bsz, length, n_heads, head_dim))
            x0, z = _hyena_pre(proj3, conv_c[l], col_hy)
            hcat = _hyena_filters(length, filt_w1[l], filt_b1[l], filt_freq[l], filt_w2[l], filt_b2[l], filt_w3[l])
            fr, fi = _filter_spectrum(cosm, sinm, hcat)
            yr, yi = _dft_forward(cosm, sinm, z, fr, fi)
            y_hy = _dft_inverse(cosm, sinm_t, yr, yi, x0, z, hyena_bias[l])
            merged = _merge(y_sc.reshape(-1, sc_w), y_na.reshape(-1, na_w), y_hy.reshape(-1, hy_w),
                            proj, col_gate, w_up_a_b[l], w_up_b_b[l], w_up_c_b[l])
            x2, h2 = _matmul_epilogue(merged, w_out_b[l], x2, norm_gains[l, 1], mod_g[l], 2, length,
                                      (norm_gains[l, 2], mod_g[l], 4, 3))
            hidden = _ffn_up(h2, w_gate_b[l], w_upf_b[l])
            if l + 1 < depth:
                x2, h = _matmul_epilogue(hidden, w_down_b[l], x2, norm_gains[l, 3], mod_g[l], 5, length,
                                         (norm_gains[l + 1, 0], mod_g[l + 1], 1, 0))
            else:
                x2 = _matmul_epilogue(hidden, w_down_b[l], x2, norm_gains[l, 3], mod_g[l], 5, length, None)
        return x2.reshape(bsz, length, d), ks, vs

    y_prompt, ks, vs = run_group(x_prompt, mod[:, 0:1], latent=False)
    y_sample, _, _ = run_group(x_sample, mod[:, 1:1 + c.shape[0]], latent=True)
    return (y_prompt, y_sample, jnp.stack(ks, axis=1), jnp.stack(vs, axis=1))
```

```python
import functools
import math

import numpy as np
import jax
import jax.numpy as jnp
from jax import lax
from jax.experimental import pallas as pl
from jax.experimental.pallas import tpu as pltpu

F32 = jnp.float32
BF16 = jnp.bfloat16

GRID_W = 64
HY_FAST = 0.3
HY_SLOW = 1.5
HY_TARGET = 1e-2
RMS_EPS = 1e-6
N_MOD = 6
N_BRANCH = 3

V7X_VMEM_BYTES = 64 * 1024 * 1024
V7X_VMEM_RESERVED_BYTES = 6 * 1024 * 1024
V7X_LANES = 128
V7X_SUBLANES = 8
MOD_ROWS_PAD = 16

TM_MM = 1024
TN_MM = 1024
TN_HALF = 512
TM_EPI = 512
EPI_K_TILES = (1024, 512, 256, 128)
EPI_ROW_CHUNKS = 8
EPI_COL_CHUNK = 1024
EPI_COL_PIECE = 512
TM_DFT = 512
TN_DFT = 512
TC_CONV = 128
FFN_PAD = 1024
HEADS_PER_STEP_CTX = 8
HEADS_PER_STEP_NAT = 2
NAT_CTX_CHUNK = 512
NAT_ROW_UNROLL = 4
DFT_SPLIT = 64


def _tile(dim, pref):
    return pref if dim % pref == 0 else dim


def _nbytes(shape, dtype):
    return int(np.prod(shape)) * jnp.dtype(dtype).itemsize


def _params(semantics, block_bytes, temp_bytes=0):
    need = 2 * block_bytes + temp_bytes
    limit = min(V7X_VMEM_BYTES - V7X_VMEM_RESERVED_BYTES, max(need, 16 * 1024 * 1024))
    return pltpu.CompilerParams(dimension_semantics=semantics, vmem_limit_bytes=limit)


def _dot(a, b):
    return jnp.dot(a, b, preferred_element_type=F32)


def _dot_t(a, b):
    return lax.dot_general(a, b, (((1,), (1,)), ((), ())), preferred_element_type=F32)


def _rms(x, gain):
    return x * lax.rsqrt(jnp.mean(x * x, axis=-1, keepdims=True) + RMS_EPS) * gain


def _mod_kernel(c_ref, w_ref, b_ref, o_ref):
    a = jax.nn.silu(c_ref[...]).astype(BF16)
    o_ref[...] = _dot(a, w_ref[...].astype(BF16)) + b_ref[...]


def _modulation(c_rows, w_mod, b_mod):
    depth, d, n = w_mod.shape
    tn = _tile(n, TN_HALF)
    blocks = _nbytes((MOD_ROWS_PAD, d), F32) + _nbytes((d, tn), F32) + _nbytes((MOD_ROWS_PAD + 1, tn), F32)
    return pl.pallas_call(
        _mod_kernel,
        out_shape=jax.ShapeDtypeStruct((depth, MOD_ROWS_PAD, n), F32),
        grid=(depth, n // tn),
        in_specs=[pl.BlockSpec((MOD_ROWS_PAD, d), lambda l, j: (0, 0)),
                  pl.BlockSpec((None, d, tn), lambda l, j: (l, 0, j)),
                  pl.BlockSpec((None, 1, tn), lambda l, j: (l, 0, j))],
        out_specs=pl.BlockSpec((None, MOD_ROWS_PAD, tn), lambda l, j: (l, 0, j)),
        compiler_params=_params(("arbitrary", "arbitrary"), blocks, _nbytes((d, tn), BF16)),
        name="modulation",
    )(c_rows, w_mod, b_mod.reshape(depth, 1, n))


def _row_tiling(mod_l, total_rows, rows_per_batch, pref):
    shared = mod_l.shape[0] == 1
    tm = _tile(total_rows if shared else rows_per_batch, pref)
    return tm, (total_rows if shared else rows_per_batch) // tm


def _vec_spec(d, which, tiles_per_batch):
    return pl.BlockSpec((None, None, 1, d), lambda i: (i // tiles_per_batch, which, 0, 0))


def _prenorm_kernel(x_ref, g_ref, sc_ref, sh_ref, h_ref):
    y = _rms(x_ref[...], g_ref[...])
    h_ref[...] = (y * (1.0 + sc_ref[...]) + sh_ref[...]).astype(BF16)


def _prenorm(x2, gain, mod_l, rows_per_batch):
    t, d = x2.shape
    tm, tpb = _row_tiling(mod_l, t, rows_per_batch, TM_EPI)
    blocks = _nbytes((tm, d), F32) + _nbytes((tm, d), BF16) + 3 * _nbytes((1, d), F32)
    return pl.pallas_call(
        _prenorm_kernel,
        out_shape=jax.ShapeDtypeStruct((t, d), BF16),
        grid=(t // tm,),
        in_specs=[pl.BlockSpec((tm, d), lambda i: (i, 0)),
                  pl.BlockSpec((1, d), lambda i: (0, 0)),
                  _vec_spec(d, 1, tpb),
                  _vec_spec(d, 0, tpb)],
        out_specs=pl.BlockSpec((tm, d), lambda i: (i, 0)),
        compiler_params=_params(("arbitrary",), blocks, 2 * _nbytes((tm, d), F32)),
        name="prenorm",
    )(x2, gain.reshape(1, d), mod_l, mod_l)


def _mm_kernel(a_ref, b_ref, o_ref):
    o_ref[...] = _dot(a_ref[...], b_ref[...]).astype(o_ref.dtype)


def _matmul(a, b, out_dtype):
    m, k = a.shape
    n = b.shape[1]
    tm, tn = _tile(m, TM_MM), _tile(n, TN_MM)
    blocks = _nbytes((tm, k), BF16) + _nbytes((k, tn), BF16) + _nbytes((tm, tn), out_dtype)
    return pl.pallas_call(
        _mm_kernel,
        out_shape=jax.ShapeDtypeStruct((m, n), out_dtype),
        grid=(m // tm, n // tn),
        in_specs=[pl.BlockSpec((tm, k), lambda i, j: (i, 0)),
                  pl.BlockSpec((k, tn), lambda i, j: (0, j))],
        out_specs=pl.BlockSpec((tm, tn), lambda i, j: (i, j)),
        compiler_params=_params(("arbitrary", "arbitrary"), blocks, 3 * _nbytes((tm, tn), F32)),
        name="in_proj",
    )(a, b)


def _dwconv3(s, w):
    n = s.shape[0]
    row = lax.broadcasted_iota(jnp.int32, s.shape, 0)
    prev = jnp.where(row == 0, 0.0, pltpu.roll(s, 1, 0))
    nxt = jnp.where(row == n - 1, 0.0, pltpu.roll(s, n - 1, 0))
    return prev * w[0:1, :] + s * w[1:2, :] + nxt * w[2:3, :]


def _sconv_kernel(b_ref, c_ref, x_ref, w_ref, o_ref):
    s = c_ref[...].astype(F32) * x_ref[...].astype(F32)
    o_ref[...] = (b_ref[...].astype(F32) * _dwconv3(s, w_ref[...])).astype(BF16)


def _short_conv(proj3, conv_w, col0):
    bsz, length, _ = proj3.shape
    width = conv_w.shape[1]
    tc = _tile(width, TC_CONV)
    nb = width // tc
    base = col0 // tc
    blocks = 3 * _nbytes((length, tc), proj3.dtype) + _nbytes((3, tc), F32) + _nbytes((length, tc), BF16)

    def col(g):
        return pl.BlockSpec((None, length, tc), lambda b, j: (b, 0, base + g * nb + j))

    return pl.pallas_call(
        _sconv_kernel,
        out_shape=jax.ShapeDtypeStruct((bsz, length, width), BF16),
        grid=(bsz, nb),
        in_specs=[col(0), col(1), col(2), pl.BlockSpec((3, tc), lambda b, j: (0, j))],
        out_specs=pl.BlockSpec((None, length, tc), lambda b, j: (b, 0, j)),
        compiler_params=_params(("arbitrary", "arbitrary"), blocks, 6 * _nbytes((length, tc), F32)),
        name="short_conv",
    )(proj3, proj3, proj3, conv_w)


def _hyena_pre_kernel(p0_ref, p1_ref, p2_ref, w0_ref, w1_ref, w2_ref, x0_ref, z_ref):
    x0_ref[...] = _dwconv3(p0_ref[...].astype(F32), w0_ref[...]).astype(BF16)
    x1 = _dwconv3(p1_ref[...].astype(F32), w1_ref[...])
    v = _dwconv3(p2_ref[...].astype(F32), w2_ref[...])
    z_ref[...] = (x1 * v).astype(BF16)


def _hyena_pre(proj3, conv_w, col0):
    bsz, length, _ = proj3.shape
    width = conv_w.shape[1] // 3
    tc = _tile(width, TC_CONV)
    nb = width // tc
    base = col0 // tc
    blocks = (3 * _nbytes((length, tc), proj3.dtype) + 3 * _nbytes((3, tc), F32)
              + 2 * _nbytes((length, tc), BF16))

    def col(g):
        return pl.BlockSpec((None, length, tc), lambda b, j: (b, 0, base + g * nb + j))

    def wcol(g):
        return pl.BlockSpec((3, tc), lambda b, j: (0, g * nb + j))

    out = jax.ShapeDtypeStruct((bsz, length, width), BF16)
    ospec = pl.BlockSpec((None, length, tc), lambda b, j: (b, 0, j))
    return pl.pallas_call(
        _hyena_pre_kernel,
        out_shape=(out, out),
        grid=(bsz, nb),
        in_specs=[col(0), col(1), col(2), wcol(0), wcol(1), wcol(2)],
        out_specs=(ospec, ospec),
        compiler_params=_params(("arbitrary", "arbitrary"), blocks, 8 * _nbytes((length, tc), F32)),
        name="hyena_pre",
    )(proj3, proj3, proj3, conv_w, conv_w, conv_w)


def _softmax_pv(parts):
    m = functools.reduce(jnp.maximum, [jnp.max(s, axis=-1, keepdims=True) for s, _ in parts])
    ps = [jnp.exp(s - m) for s, _ in parts]
    denom = functools.reduce(jnp.add, [jnp.sum(p, axis=-1, keepdims=True) for p in ps])
    acc = functools.reduce(jnp.add, [_dot(p.astype(BF16), v) for p, (_, v) in zip(ps, parts)])
    return acc / denom


def _ctx_attn_kernel(q_ref, k_ref, v_ref, o_ref, *, heads, head_dim):
    scale = head_dim ** -0.5
    for h in range(heads):
        sl = slice(h * head_dim, (h + 1) * head_dim)
        q = q_ref[:, sl].astype(BF16)
        k = k_ref[:, sl].astype(BF16)
        v = v_ref[:, sl].astype(BF16)
        o_ref[:, sl] = _softmax_pv([(_dot_t(q, k) * scale, v)]).astype(BF16)


def _context_attention(proj3, col0, n_heads, head_dim):
    bsz, length, _ = proj3.shape
    hp = _heads_per_step(n_heads, head_dim, col0, HEADS_PER_STEP_CTX)
    bw = hp * head_dim
    nb = n_heads // hp
    base = col0 // bw
    blocks = 3 * _nbytes((length, bw), proj3.dtype) + _nbytes((length, bw), BF16)

    def col(g):
        return pl.BlockSpec((None, length, bw), lambda b, j: (b, 0, base + g * nb + j))

    return pl.pallas_call(
        functools.partial(_ctx_attn_kernel, heads=hp, head_dim=head_dim),
        out_shape=jax.ShapeDtypeStruct((bsz, length, n_heads * head_dim), BF16),
        grid=(bsz, nb),
        in_specs=[col(0), col(1), col(2)],
        out_specs=pl.BlockSpec((None, length, bw), lambda b, j: (b, 0, j)),
        compiler_params=_params(("arbitrary", "arbitrary"), blocks, 8 * _nbytes((length, length), F32)),
        name="context_attention",
    )(proj3, proj3, proj3)


def _heads_per_step(n_heads, head_dim, col0, cap):
    return max(h for h in range(1, cap + 1) if n_heads % h == 0 and col0 % (h * head_dim) == 0)


def _nat_kernel(q_ref, k_ref, v_ref, kc_ref, vc_ref, bias_ref, o_ref, mc_ref, lc_ref, oc_ref,
                *, rows, kh, heads, head_dim):
    scale = head_dim ** -0.5
    length = rows * GRID_W
    cch = _tile(length, NAT_CTX_CHUNK)
    unroll = NAT_ROW_UNROLL if rows % NAT_ROW_UNROLL == 0 else 1
    for h in range(heads):
        hs = slice(h * head_dim, (h + 1) * head_dim)
        kc = kc_ref[:, hs].astype(BF16)
        vc = vc_ref[:, hs].astype(BF16)

        def ctx_chunk(ci, carry):
            rws = pl.ds(pl.multiple_of(ci * cch, cch), cch)
            s = _dot_t(q_ref[rws, hs], kc) * scale
            m = jnp.max(s, axis=-1, keepdims=True)
            p = jnp.exp(s - m)
            mc_ref[rws, :] = m
            lc_ref[rws, :] = jnp.sum(p, axis=-1, keepdims=True)
            oc_ref[rws, :] = _dot(p.astype(BF16), vc)
            return carry

        lax.fori_loop(0, length // cch, ctx_chunk, 0)

        def row_group(g, carry):
            qrows, wins, offs = [], [], []
            for u in range(unroll):
                r = g * unroll + u
                rs = jnp.clip(r - kh // 2, 0, rows - kh)
                qrows.append(pl.ds(pl.multiple_of(r * GRID_W, GRID_W), GRID_W))
                wins.append(pl.ds(pl.multiple_of(rs * GRID_W, GRID_W), kh * GRID_W))
                offs.append(rs - r + kh - 1)
            scores = [_dot_t(q_ref[qr, hs], k_ref[w, hs]) * scale + bias_ref[h, o]
                      for qr, w, o in zip(qrows, wins, offs)]
            probs, wcs, denoms = [], [], []
            for qr, s in zip(qrows, scores):
                mc = mc_ref[qr, :]
                m = jnp.maximum(jnp.max(s, axis=-1, keepdims=True), mc)
                p = jnp.exp(s - m)
                wc = jnp.exp(mc - m)
                probs.append(p.astype(BF16))
                wcs.append(wc)
                denoms.append(jnp.sum(p, axis=-1, keepdims=True) + lc_ref[qr, :] * wc)
            accs = [_dot(p, v_ref[w, hs]) for p, w in zip(probs, wins)]
            for qr, acc, wc, denom in zip(qrows, accs, wcs, denoms):
                o_ref[qr, hs] = ((acc + oc_ref[qr, :] * wc) / denom).astype(BF16)
            return carry

        lax.fori_loop(0, rows // unroll, row_group, 0)


def _nat_bias_strips(rpb, kh, kw):
    n_heads = rpb.shape[0]
    qc = np.arange(GRID_W)[:, None]
    kc = np.arange(GRID_W)[None, :]
    cstart = np.clip(qc - kw // 2, 0, GRID_W - kw)
    ok = (kc >= cstart) & (kc < cstart + kw)
    padded = jnp.pad(rpb, ((0, 0), (0, 0), (GRID_W - kw, GRID_W - kw)))
    toep = jnp.stack([padded[:, :, GRID_W - 1 - q:2 * GRID_W - 1 - q] for q in range(GRID_W)], axis=2)
    toep = jnp.where(ok[None, None], toep, -jnp.inf)
    return jnp.stack([jnp.transpose(toep[:, o:o + kh], (0, 2, 1, 3)).reshape(n_heads, GRID_W, kh * GRID_W)
                      for o in range(kh)], axis=1)


def _neighbourhood_attention(proj3, col0, cache_k4, cache_v4, layer, strips, n_heads, head_dim):
    bsz, length, _ = proj3.shape
    rows = length // GRID_W
    kh = strips.shape[1]
    past = cache_k4.shape[2]
    hp = _heads_per_step(n_heads, head_dim, col0, HEADS_PER_STEP_NAT)
    bw = hp * head_dim
    nb = n_heads // hp
    base = col0 // bw
    blocks = (3 * _nbytes((length, bw), proj3.dtype) + 2 * _nbytes((past, bw), F32)
              + _nbytes((hp,) + strips.shape[1:], F32) + _nbytes((length, bw), BF16))
    scratch = [pltpu.VMEM((length, 1), F32), pltpu.VMEM((length, 1), F32), pltpu.VMEM((length, head_dim), F32)]
    temps = (2 * _nbytes((length, V7X_LANES), F32) + _nbytes((length, head_dim), F32)
             + 8 * _nbytes((_tile(length, NAT_CTX_CHUNK), past), F32)
             + 8 * NAT_ROW_UNROLL * _nbytes((GRID_W, kh * GRID_W), F32))

    def col(g):
        return pl.BlockSpec((None, length, bw), lambda b, j: (b, 0, base + g * nb + j))

    cache_spec = pl.BlockSpec((None, None, past, bw), lambda b, j: (b, layer, 0, j))
    return pl.pallas_call(
        functools.partial(_nat_kernel, rows=rows, kh=kh, heads=hp, head_dim=head_dim),
        out_shape=jax.ShapeDtypeStruct((bsz, length, n_heads * head_dim), BF16),
        grid=(bsz, nb),
        in_specs=[col(0), col(1), col(2), cache_spec, cache_spec,
                  pl.BlockSpec((hp,) + strips.shape[1:], lambda b, j: (j, 0, 0, 0))],
        out_specs=pl.BlockSpec((None, length, bw), lambda b, j: (b, 0, j)),
        scratch_shapes=scratch,
        compiler_params=_params(("arbitrary", "arbitrary"), blocks, temps),
        name="neighbourhood_attention",
    )(proj3, proj3, proj3, cache_k4, cache_v4, strips)


def _dft_matrices(length):
    n = 2 * length
    split = min(DFT_SPLIT, length)
    s = np.arange(length, dtype=np.int64)
    pa = ((np.arange(length // split, dtype=np.int64)[:, None] * split * s[None, :]) % n).astype(np.int32)
    pb = ((np.arange(split, dtype=np.int64)[:, None] * s[None, :]) % n).astype(np.int32)
    ta = jnp.asarray(pa).astype(F32) * (2.0 * math.pi / n)
    tb = jnp.asarray(pb).astype(F32) * (2.0 * math.pi / n)
    ca, sa, cb, sb = jnp.cos(ta), jnp.sin(ta), jnp.cos(tb), jnp.sin(tb)
    cosm = (ca[:, None, :] * cb[None, :, :] - sa[:, None, :] * sb[None, :, :]).reshape(length, length)
    msin = -(sa[:, None, :] * cb[None, :, :] + ca[:, None, :] * sb[None, :, :]).reshape(length, length)
    k_idx = lax.broadcasted_iota(jnp.int32, (length, length), 0)
    s_idx = lax.broadcasted_iota(jnp.int32, (length, length), 1)
    sinm = jnp.where(k_idx == 0, jnp.where(s_idx % 2 == 0, 1.0, -1.0), msin)
    sinm_t = jnp.where(s_idx == 0, jnp.where(k_idx % 2 == 0, 1.0, -1.0), msin)
    return cosm.astype(BF16), sinm.astype(BF16), sinm_t.astype(BF16)


def _filter_kernel(bands_ref, w1_ref, b1_ref, f_ref, w2_ref, b2_ref, w3_ref, dl_ref, o_ref,
                   *, length, tl, emb_bands, width):
    hi = lax.Precision.HIGHEST
    pos = (lax.broadcasted_iota(jnp.int32, (tl, V7X_LANES), 0) + pl.program_id(0) * tl).astype(F32)
    lane = lax.broadcasted_iota(jnp.int32, (tl, V7X_LANES), 1)
    t = pos * (1.0 / (length - 1))
    ang = ((2.0 * math.pi / length) * pos) * bands_ref[...]
    feat = jnp.where(lane == 0, t,
                     jnp.where(lane <= emb_bands, jnp.cos(ang),
                               jnp.where(lane <= 2 * emb_bands, -jnp.sin(ang), 0.0)))
    h = jnp.sin(f_ref[0:1, :] * (jnp.dot(feat, w1_ref[...], precision=hi, preferred_element_type=F32)
                                 + b1_ref[...]))
    h = jnp.sin(f_ref[1:2, :] * (jnp.dot(h, w2_ref[...], precision=hi, preferred_element_type=F32)
                                 + b2_ref[...]))
    h = jnp.dot(h, w3_ref[...], precision=hi, preferred_element_type=F32)
    decay = jnp.exp(-(t[:, 0:1]) * dl_ref[...])
    col = lax.broadcasted_iota(jnp.int32, h.shape, 1)
    first = (pos[:, 0:1] == 0.0) & (col >= width)
    o_ref[...] = jnp.where(first, 0.0, h * decay).astype(BF16)


def _hyena_filters(length, w1, b1, freq, w2, b2, w3):
    emb, fo = w1.shape
    width = w3.shape[1] // 2
    emb_bands = (emb - 1) // 2
    tl = _tile(length, TM_DFT)
    bands = jnp.linspace(1e-4, emb_bands - 1, emb_bands, dtype=F32)
    bands_row = jnp.zeros((1, V7X_LANES), F32).at[0, 1:1 + emb_bands].set(bands)
    bands_row = bands_row.at[0, 1 + emb_bands:1 + 2 * emb_bands].set(bands)
    w1p = jnp.zeros((V7X_LANES, fo), F32).at[:emb].set(w1)
    deltas = jnp.abs(jnp.linspace(math.log(HY_TARGET) / HY_SLOW, math.log(HY_TARGET) / HY_FAST, width, dtype=F32))
    dl = jnp.concatenate([deltas, deltas]).reshape(1, 2 * width)
    full = lambda shape: pl.BlockSpec(shape, lambda i: (0,) * len(shape))
    blocks = _nbytes((tl, 2 * width), BF16) + _nbytes((fo + 2, 2 * width), F32) + _nbytes((V7X_LANES + fo, fo), F32)
    return pl.pallas_call(
        functools.partial(_filter_kernel, length=length, tl=tl, emb_bands=emb_bands, width=width),
        out_shape=jax.ShapeDtypeStruct((length, 2 * width), BF16),
        grid=(length // tl,),
        in_specs=[full((1, V7X_LANES)), full((V7X_LANES, fo)), full((1, fo)), full((2, fo)),
                  full((fo, fo)), full((1, fo)), full((fo, 2 * width)), full((1, 2 * width))],
        out_specs=pl.BlockSpec((tl, 2 * width), lambda i: (i, 0)),
        compiler_params=_params(("arbitrary",), blocks, 6 * _nbytes((tl, 2 * width), F32)),
        name="hyena_filters",
    )(bands_row, w1p, b1.reshape(1, fo), freq, w2, b2.reshape(1, fo), w3, dl)


def _spectrum_kernel(c_ref, s_ref, hf_ref, hb_ref, fr_ref, fi_ref, *, tm, norm):
    cm, sm, hf, hb = c_ref[...], s_ref[...], hf_ref[...], hb_ref[...]
    row = lax.broadcasted_iota(jnp.int32, fr_ref.shape, 0) + pl.program_id(0) * tm
    packed = row == 0
    wgt = jnp.where(packed, 0.5 * norm, norm)
    bi = _dot(sm, hb)
    fr_ref[...] = (_dot(cm, hf) + _dot(cm, hb)) * wgt
    fi_ref[...] = (_dot(sm, hf) + jnp.where(packed, bi, -bi)) * wgt


def _filter_spectrum(cosm, sinm, hcat):
    length = cosm.shape[0]
    width = hcat.shape[1] // 2
    tm, tn = _tile(length, TM_DFT), _tile(width, TN_DFT)
    nb = width // tn
    blocks = 2 * _nbytes((tm, length), BF16) + 2 * _nbytes((length, tn), BF16) + 2 * _nbytes((tm, tn), F32)
    out = jax.ShapeDtypeStruct((length, width), F32)
    ospec = pl.BlockSpec((tm, tn), lambda i, j: (i, j))
    return pl.pallas_call(
        functools.partial(_spectrum_kernel, tm=tm, norm=1.0 / length),
        out_shape=(out, out),
        grid=(length // tm, nb),
        in_specs=[pl.BlockSpec((tm, length), lambda i, j: (i, 0)),
                  pl.BlockSpec((tm, length), lambda i, j: (i, 0)),
                  pl.BlockSpec((length, tn), lambda i, j: (0, j)),
                  pl.BlockSpec((length, tn), lambda i, j: (0, nb + j))],
        out_specs=(ospec, ospec),
        compiler_params=_params(("arbitrary", "arbitrary"), blocks, 16 * _nbytes((tm, tn), F32)),
        name="filter_spectrum",
    )(cosm, sinm, hcat, hcat)


def _dft_fwd_kernel(c_ref, s_ref, z_ref, fr_ref, fi_ref, yr_ref, yi_ref, *, tm):
    z = z_ref[...]
    zr, zi = _dot(c_ref[...], z), _dot(s_ref[...], z)
    fr, fi = fr_ref[...], fi_ref[...]
    packed = (lax.broadcasted_iota(jnp.int32, zr.shape, 0) + pl.program_id(0) * tm) == 0
    yr_ref[...] = jnp.where(packed, zr * fr, zr * fr - zi * fi).astype(BF16)
    yi_ref[...] = jnp.where(packed, zi * fi, zr * fi + zi * fr).astype(BF16)


def _dft_forward(cosm, sinm, z, fr, fi):
    bsz, length, width = z.shape
    tm, tn = _tile(length, TM_DFT), _tile(width, TN_DFT)
    blocks = (2 * _nbytes((tm, length), BF16) + _nbytes((length, tn), BF16) + 2 * _nbytes((tm, tn), F32)
              + 2 * _nbytes((tm, tn), BF16))
    out = jax.ShapeDtypeStruct((bsz, length, width), BF16)
    ospec = pl.BlockSpec((None, tm, tn), lambda i, b, j: (b, i, j))
    return pl.pallas_call(
        functools.partial(_dft_fwd_kernel, tm=tm),
        out_shape=(out, out),
        grid=(length // tm, bsz, width // tn),
        in_specs=[pl.BlockSpec((tm, length), lambda i, b, j: (i, 0)),
                  pl.BlockSpec((tm, length), lambda i, b, j: (i, 0)),
                  pl.BlockSpec((None, length, tn), lambda i, b, j: (b, 0, j)),
                  pl.BlockSpec((tm, tn), lambda i, b, j: (i, j)),
                  pl.BlockSpec((tm, tn), lambda i, b, j: (i, j))],
        out_specs=(ospec, ospec),
        compiler_params=_params(("arbitrary",) * 3, blocks, 16 * _nbytes((tm, tn), F32)),
        name="dft_forward",
    )(cosm, sinm, z, fr, fi)


def _dft_inv_kernel(c_ref, st_ref, yr_ref, yi_ref, x0_ref, z_ref, b_ref, o_ref):
    y = _dot(c_ref[...], yr_ref[...]) + _dot(st_ref[...], yi_ref[...])
    y = y + z_ref[...].astype(F32) * b_ref[...]
    o_ref[...] = (x0_ref[...].astype(F32) * y).astype(BF16)


def _dft_inverse(cosm, sinm_t, yr, yi, x0, z, bias):
    bsz, length, width = z.shape
    tm, tn = _tile(length, TM_DFT), _tile(width, TN_DFT)
    blocks = (2 * _nbytes((tm, length), BF16) + 2 * _nbytes((length, tn), BF16)
              + 3 * _nbytes((tm, tn), BF16) + _nbytes((1, tn), F32))
    tile = pl.BlockSpec((None, tm, tn), lambda i, b, j: (b, i, j))
    panel = pl.BlockSpec((None, length, tn), lambda i, b, j: (b, 0, j))
    return pl.pallas_call(
        _dft_inv_kernel,
        out_shape=jax.ShapeDtypeStruct((bsz, length, width), BF16),
        grid=(length // tm, bsz, width // tn),
        in_specs=[pl.BlockSpec((tm, length), lambda i, b, j: (i, 0)),
                  pl.BlockSpec((tm, length), lambda i, b, j: (i, 0)),
                  panel, panel, tile, tile,
                  pl.BlockSpec((1, tn), lambda i, b, j: (0, j))],
        out_specs=tile,
        compiler_params=_params(("arbitrary",) * 3, blocks, 16 * _nbytes((tm, tn), F32)),
        name="dft_inverse",
    )(cosm, sinm_t, yr, yi, x0, z, bias.reshape(1, width))


def _merge_kernel(ya_ref, yb_ref, yc_ref, ga_ref, gb_ref, gc_ref, wa_ref, wb_ref, wc_ref, o_ref):
    m = jax.nn.sigmoid(ga_ref[...].astype(F32)) * _dot(ya_ref[...], wa_ref[...])
    m = m + jax.nn.sigmoid(gb_ref[...].astype(F32)) * _dot(yb_ref[...], wb_ref[...])
    m = m + jax.nn.sigmoid(gc_ref[...].astype(F32)) * _dot(yc_ref[...], wc_ref[...])
    o_ref[...] = m.astype(BF16)


def _merge(ya, yb, yc, proj, gate_col0, wa, wb, wc):
    t = ya.shape[0]
    d = wa.shape[1]
    tm, tn = _tile(t, TM_MM), _tile(d, TN_HALF)
    nb = d // tn
    base = gate_col0 // tn
    ka, kb, kc = ya.shape[1], yb.shape[1], yc.shape[1]
    blocks = (_nbytes((tm, ka + kb + kc), BF16) + 3 * _nbytes((tm, tn), proj.dtype)
              + _nbytes((ka + kb + kc, tn), BF16) + _nbytes((tm, tn), BF16))

    def gate(g):
        return pl.BlockSpec((tm, tn), lambda i, j: (i, base + g * nb + j))

    def panel(k):
        return pl.BlockSpec((tm, k), lambda i, j: (i, 0))

    def wcol(k):
        return pl.BlockSpec((k, tn), lambda i, j: (0, j))

    return pl.pallas_call(
        _merge_kernel,
        out_shape=jax.ShapeDtypeStruct((t, d), BF16),
        grid=(t // tm, nb),
        in_specs=[panel(ka), panel(kb), panel(kc), gate(0), gate(1), gate(2), wcol(ka), wcol(kb), wcol(kc)],
        out_specs=pl.BlockSpec((tm, tn), lambda i, j: (i, j)),
        compiler_params=_params(("arbitrary", "arbitrary"), blocks, 16 * _nbytes((tm, tn), F32)),
        name="gated_merge",
    )(ya, yb, yc, proj, proj, proj, wa, wb, wc)


def _lane_partial_sum(v):
    return functools.reduce(jnp.add, [v[:, t * V7X_LANES:(t + 1) * V7X_LANES]
                                      for t in range(v.shape[1] // V7X_LANES)])


def _mm_epilogue_kernel(a_ref, b_ref, x_ref, gpost_ref, gate_ref, *rest, n_tiles, n_chunks, with_next):
    if with_next:
        gnext_ref, sc_ref, sh_ref, xo_ref, h_ref, acc_ref = rest
    else:
        xo_ref, acc_ref = rest
    i = pl.program_id(0)
    k = pl.program_id(1)
    _, tm, d = acc_ref.shape
    rc = tm // n_chunks
    cw = _tile(d, EPI_COL_CHUNK)
    ew = _tile(d, EPI_COL_PIECE)
    chunk = jnp.minimum(k, n_chunks - 1)

    @pl.when((i == 0) & (k == 0))
    def _():
        acc_ref[...] = jnp.zeros_like(acc_ref)

    def accumulate(slot):
        a = a_ref[...]
        for c in range(d // cw):
            cols = slice(c * cw, (c + 1) * cw)
            part = _dot(a, b_ref[:, cols])
            acc_ref[slot, :, cols] = jnp.where(k == 0, part, acc_ref[slot, :, cols] + part)

    def epilogue(slot):
        rows = pl.ds(pl.multiple_of(chunk * rc, rc), rc)
        pieces = [slice(c * ew, (c + 1) * ew) for c in range(d // ew)]
        ssq = functools.reduce(jnp.add, [_lane_partial_sum(jnp.square(acc_ref[slot, rows, cs]))
                                         for cs in pieces])
        rstd = lax.rsqrt(jnp.sum(ssq, axis=-1, keepdims=True) / d + RMS_EPS)
        ssq = jnp.zeros((rc, V7X_LANES), F32)
        for cs in pieces:
            y = acc_ref[slot, rows, cs] * rstd * gpost_ref[:, cs]
            xn = x_ref[:, cs] + gate_ref[:, cs] * y
            xo_ref[:, cs] = xn
            ssq = ssq + _lane_partial_sum(jnp.square(xn))
        if with_next:
            rstd = lax.rsqrt(jnp.sum(ssq, axis=-1, keepdims=True) / d + RMS_EPS)
            for cs in pieces:
                hn = xo_ref[:, cs] * rstd * gnext_ref[:, cs]
                h_ref[:, cs] = (hn * (1.0 + sc_ref[:, cs]) + sh_ref[:, cs]).astype(BF16)

    @pl.when(i == 0)
    def _():
        accumulate(0)

    inner = (i > 0) & (i < n_tiles)

    @pl.when(inner & (i % 2 == 0))
    def _():
        accumulate(0)
        epilogue(1)

    @pl.when(inner & (i % 2 == 1))
    def _():
        accumulate(1)
        epilogue(0)

    @pl.when(i == n_tiles)
    def _():
        epilogue((n_tiles - 1) % 2)


def _matmul_epilogue(a, b, x2, gain_post, mod_l, gate_idx, rows_per_batch, nxt):
    t, kdim = a.shape
    d = b.shape[1]
    tm, tpb = _row_tiling(mod_l, t, rows_per_batch, TM_EPI)
    n_tiles = t // tm
    fits = [w for w in EPI_K_TILES if kdim % w == 0]
    tk = next((w for w in fits if kdim // w >= EPI_ROW_CHUNKS), fits[-1])
    nk = kdim // tk
    n_chunks = max(n for n in range(1, min(nk, EPI_ROW_CHUNKS) + 1) if tm % (n * 2 * V7X_SUBLANES) == 0)
    rc = tm // n_chunks

    def prev_tile(i):
        return jnp.maximum(i - 1, 0)

    def chunk_index(i, k):
        return jnp.where(i == 0, 0, (i - 1) * n_chunks + jnp.minimum(k, n_chunks - 1))

    def k_index(i, k):
        return jnp.where(i == n_tiles, nk - 1, k)

    def vec(which):
        return pl.BlockSpec((None, None, 1, d), lambda i, k: (prev_tile(i) // tpb, which, 0, 0))

    row = pl.BlockSpec((1, d), lambda i, k: (0, 0))
    chunk_rows = pl.BlockSpec((rc, d), lambda i, k: (chunk_index(i, k), 0))
    in_specs = [pl.BlockSpec((tm, tk), lambda i, k: (jnp.minimum(i, n_tiles - 1), k_index(i, k))),
                pl.BlockSpec((tk, d), lambda i, k: (k_index(i, k), 0)),
                chunk_rows, row, vec(gate_idx)]
    args = [a, b, x2, gain_post.reshape(1, d), mod_l]
    blocks = (_nbytes((tm, tk), BF16) + _nbytes((tk, d), BF16) + 2 * _nbytes((rc, d), F32)
              + 5 * _nbytes((1, d), F32))
    if nxt is None:
        out_shape = jax.ShapeDtypeStruct((t, d), F32)
        out_specs = chunk_rows
    else:
        gain_next, mod_next, sc_idx, sh_idx = nxt
        in_specs += [row, vec(sc_idx), vec(sh_idx)]
        args += [gain_next.reshape(1, d), mod_next, mod_next]
        out_shape = (jax.ShapeDtypeStruct((t, d), F32), jax.ShapeDtypeStruct((t, d), BF16))
        out_specs = (chunk_rows, chunk_rows)
        blocks += _nbytes((rc, d), BF16)
    temps = (_nbytes((2, tm, d), F32) + 3 * _nbytes((tm, _tile(d, EPI_COL_CHUNK)), F32)
             + 8 * _nbytes((rc, _tile(d, EPI_COL_PIECE)), F32))
    return pl.pallas_call(
        functools.partial(_mm_epilogue_kernel, n_tiles=n_tiles, n_chunks=n_chunks, with_next=nxt is not None),
        out_shape=out_shape,
        grid=(n_tiles + 1, nk),
        in_specs=in_specs,
        out_specs=out_specs,
        scratch_shapes=[pltpu.VMEM((2, tm, d), F32)],
        compiler_params=_params(("arbitrary", "arbitrary"), blocks, temps),
        name="matmul_norm_residual",
    )(*args)


def _ffn_up_kernel(h_ref, wg_ref, wu_ref, o_ref):
    h = h_ref[...]
    o_ref[...] = (jax.nn.silu(_dot(h, wg_ref[...])) * _dot(h, wu_ref[...])).astype(BF16)


def _ffn_up(h, wg, wu):
    t, d = h.shape
    n = wg.shape[1]
    tm, tn = _tile(t, TM_MM), _tile(n, TN_HALF)
    blocks = _nbytes((tm, d), BF16) + 2 * _nbytes((d, tn), BF16) + _nbytes((tm, tn), BF16)
    return pl.pallas_call(
        _ffn_up_kernel,
        out_shape=jax.ShapeDtypeStruct((t, n), BF16),
        grid=(t // tm, n // tn),
        in_specs=[pl.BlockSpec((tm, d), lambda i, j: (i, 0)),
                  pl.BlockSpec((d, tn), lambda i, j: (0, j)),
                  pl.BlockSpec((d, tn), lambda i, j: (0, j))],
        out_specs=pl.BlockSpec((tm, tn), lambda i, j: (i, j)),
        compiler_params=_params(("arbitrary", "arbitrary"), blocks, 16 * _nbytes((tm, tn), F32)),
        name="ffn_up",
    )(h, wg, wu)


def _pad_cols(w, mult):
    pad = (-w.shape[-1]) % mult
    return jnp.pad(w, ((0, 0), (0, 0), (0, pad))) if pad else w


def _pad_rows(w, mult):
    pad = (-w.shape[-2]) % mult
    return jnp.pad(w, ((0, 0), (0, pad), (0, 0))) if pad else w


def kernel(x_prompt, x_sample, cache_k, cache_v, c, c_ctx, w_mod, b_mod, norm_gains, w_in, conv_a, w_up_a, na_rpb, w_up_b, conv_c, filt_w1, filt_b1, filt_freq, filt_w2, filt_b2, filt_w3, hyena_bias, w_up_c, w_out, w_ffn_gate, w_ffn_up, w_ffn_down):
    depth, d, _ = w_mod.shape
    n_heads, head_dim = cache_k.shape[3], cache_k.shape[4]
    sc_w = conv_a.shape[-1]
    na_w = n_heads * head_dim
    hy_w = hyena_bias.shape[-1]
    kh_full = (na_rpb.shape[2] + 1) // 2
    kw = (na_rpb.shape[3] + 1) // 2
    col_na = 3 * sc_w
    col_hy = col_na + 3 * na_w
    col_gate = col_hy + 3 * hy_w
    assert x_sample.shape[1] % GRID_W == 0 and x_sample.shape[1] // GRID_W >= kh_full
    assert 1 + c.shape[0] <= MOD_ROWS_PAD

    w_in_b = w_in.astype(BF16)
    w_up_a_b, w_up_b_b, w_up_c_b = w_up_a.astype(BF16), w_up_b.astype(BF16), w_up_c.astype(BF16)
    w_out_b = w_out.astype(BF16)
    w_gate_b = _pad_cols(w_ffn_gate.astype(BF16), FFN_PAD)
    w_upf_b = _pad_cols(w_ffn_up.astype(BF16), FFN_PAD)
    w_down_b = _pad_rows(w_ffn_down.astype(BF16), FFN_PAD)

    c_rows = jnp.zeros((MOD_ROWS_PAD, d), F32).at[0].set(c_ctx).at[1:1 + c.shape[0]].set(c)
    mod = _modulation(c_rows, w_mod, b_mod).reshape(depth, MOD_ROWS_PAD, N_MOD, 1, d)
    cache_k4 = cache_k.reshape(cache_k.shape[:3] + (na_w,))
    cache_v4 = cache_v.reshape(cache_v.shape[:3] + (na_w,))

    def run_group(x3, mod_g, latent):
        bsz, length, _ = x3.shape
        proj_dtype = BF16 if latent else F32
        cosm, sinm, sinm_t = _dft_matrices(length)
        x2 = x3.reshape(bsz * length, d)
        h = _prenorm(x2, norm_gains[0, 0], mod_g[0], length)
        ks, vs = [], []
        for l in range(depth):
            proj = _matmul(h, w_in_b[l], proj_dtype)
            proj3 = proj.reshape(bsz, length, proj.shape[1])
            y_sc = _short_conv(proj3, conv_a[l], 0)
            if latent:
                strips = _nat_bias_strips(na_rpb[l], kh_full, kw)
                y_na = _neighbourhood_attention(proj3, col_na, cache_k4, cache_v4, l, strips, n_heads, head_dim)
            else:
                y_na = _context_attention(proj3, col_na, n_heads, head_dim)
                ks.append(proj3[:, :, col_na + na_w:col_na + 2 * na_w].reshape(bsz, length, n_heads, head_dim))
                vs.append(proj3[:, :, col_na + 2 * na_w:col_hy].reshape(bsz, length, n_heads, head_dim))
            x0, z = _hyena_pre(proj3, conv_c[l], col_hy)
            hcat = _hyena_filters(length, filt_w1[l], filt_b1[l], filt_freq[l], filt_w2[l], filt_b2[l], filt_w3[l])
            fr, fi = _filter_spectrum(cosm, sinm, hcat)
            yr, yi = _dft_forward(cosm, sinm, z, fr, fi)
            y_hy = _dft_inverse(cosm, sinm_t, yr, yi, x0, z, hyena_bias[l])
            merged = _merge(y_sc.reshape(-1, sc_w), y_na.reshape(-1, na_w), y_hy.reshape(-1, hy_w),
                            proj, col_gate, w_up_a_b[l], w_up_b_b[l], w_up_c_b[l])
            x2, h2 = _matmul_epilogue(merged, w_out_b[l], x2, norm_gains[l, 1], mod_g[l], 2, length,
                                      (norm_gains[l, 2], mod_g[l], 4, 3))
            hidden = _ffn_up(h2, w_gate_b[l], w_upf_b[l])
            if l + 1 < depth:
                x2, h = _matmul_epilogue(hidden, w_down_b[l], x2, norm_gains[l, 3], mod_g[l], 5, length,
                                         (norm_gains[l + 1, 0], mod_g[l + 1], 1, 0))
            else:
                x2 = _matmul_epilogue(hidden, w_down_b[l], x2, norm_gains[l, 3], mod_g[l], 5, length, None)
        return x2.reshape(bsz, length, d), ks, vs

    y_prompt, ks, vs = run_group(x_prompt, mod[:, 0:1], latent=False)
    y_sample, _, _ = run_group(x_sample, mod[:, 1:1 + c.shape[0]], latent=True)
    return (y_prompt, y_sample, jnp.stack(ks, axis=1), jnp.stack(vs, axis=1))
```

```python
import functools
import math

import numpy as np
import jax
import jax.numpy as jnp
from jax import lax
from jax.experimental import pallas as pl
from jax.experimental.pallas import tpu as pltpu

F32 = jnp.float32
BF16 = jnp.bfloat16

GRID_W = 64
HY_FAST = 0.3
HY_SLOW = 1.5
HY_TARGET = 1e-2
RMS_EPS = 1e-6
N_MOD = 6
N_BRANCH = 3

V7X_VMEM_BYTES = 64 * 1024 * 1024
V7X_VMEM_RESERVED_BYTES = 6 * 1024 * 1024
V7X_LANES = 128
V7X_SUBLANES = 8
MOD_ROWS_PAD = 16

TM_MM = 1024
TN_MM = 1024
TN_HALF = 512
TM_EPI = 512
EPI_K_TILES = (1024, 512, 256, 128)
EPI_FULL_DEPTH_MAX = 4096
EPI_ROW_CHUNKS = 8
EPI_COL_CHUNK = 1024
EPI_COL_PIECE = 512
TM_DFT = 512
TN_DFT = 512
TC_CONV = 128
FFN_PAD = 1024
HEADS_PER_STEP_CTX = 8
HEADS_PER_STEP_NAT = 2
NAT_CTX_CHUNK = 512
NAT_BLOCK_ROWS = 4
NAT_UNROLL = 4
DFT_SPLIT = 64


def _tile(dim, pref):
    return pref if dim % pref == 0 else dim


def _nbytes(shape, dtype):
    return int(np.prod(shape)) * jnp.dtype(dtype).itemsize


def _params(semantics, block_bytes, temp_bytes=0):
    need = 2 * block_bytes + temp_bytes
    limit = min(V7X_VMEM_BYTES - V7X_VMEM_RESERVED_BYTES, max(need, 16 * 1024 * 1024))
    return pltpu.CompilerParams(dimension_semantics=semantics, vmem_limit_bytes=limit)


def _dot(a, b):
    return jnp.dot(a, b, preferred_element_type=F32)


def _dot_t(a, b):
    return lax.dot_general(a, b, (((1,), (1,)), ((), ())), preferred_element_type=F32)


def _rms(x, gain):
    return x * lax.rsqrt(jnp.mean(x * x, axis=-1, keepdims=True) + RMS_EPS) * gain


def _mod_kernel(c_ref, w_ref, b_ref, o_ref):
    a = jax.nn.silu(c_ref[...]).astype(BF16)
    o_ref[...] = _dot(a, w_ref[...].astype(BF16)) + b_ref[...]


def _modulation(c_rows, w_mod, b_mod):
    depth, d, n = w_mod.shape
    tn = _tile(n, TN_HALF)
    blocks = _nbytes((MOD_ROWS_PAD, d), F32) + _nbytes((d, tn), F32) + _nbytes((MOD_ROWS_PAD + 1, tn), F32)
    return pl.pallas_call(
        _mod_kernel,
        out_shape=jax.ShapeDtypeStruct((depth, MOD_ROWS_PAD, n), F32),
        grid=(depth, n // tn),
        in_specs=[pl.BlockSpec((MOD_ROWS_PAD, d), lambda l, j: (0, 0)),
                  pl.BlockSpec((None, d, tn), lambda l, j: (l, 0, j)),
                  pl.BlockSpec((None, 1, tn), lambda l, j: (l, 0, j))],
        out_specs=pl.BlockSpec((None, MOD_ROWS_PAD, tn), lambda l, j: (l, 0, j)),
        compiler_params=_params(("arbitrary", "arbitrary"), blocks, _nbytes((d, tn), BF16)),
        name="modulation",
    )(c_rows, w_mod, b_mod.reshape(depth, 1, n))


def _row_tiling(mod_l, total_rows, rows_per_batch, pref):
    shared = mod_l.shape[0] == 1
    tm = _tile(total_rows if shared else rows_per_batch, pref)
    return tm, (total_rows if shared else rows_per_batch) // tm


def _vec_spec(d, which, tiles_per_batch):
    return pl.BlockSpec((None, None, 1, d), lambda i: (i // tiles_per_batch, which, 0, 0))


def _prenorm_kernel(x_ref, g_ref, sc_ref, sh_ref, h_ref):
    y = _rms(x_ref[...], g_ref[...])
    h_ref[...] = (y * (1.0 + sc_ref[...]) + sh_ref[...]).astype(BF16)


def _prenorm(x2, gain, mod_l, rows_per_batch):
    t, d = x2.shape
    tm, tpb = _row_tiling(mod_l, t, rows_per_batch, TM_EPI)
    blocks = _nbytes((tm, d), F32) + _nbytes((tm, d), BF16) + 3 * _nbytes((1, d), F32)
    return pl.pallas_call(
        _prenorm_kernel,
        out_shape=jax.ShapeDtypeStruct((t, d), BF16),
        grid=(t // tm,),
        in_specs=[pl.BlockSpec((tm, d), lambda i: (i, 0)),
                  pl.BlockSpec((1, d), lambda i: (0, 0)),
                  _vec_spec(d, 1, tpb),
                  _vec_spec(d, 0, tpb)],
        out_specs=pl.BlockSpec((tm, d), lambda i: (i, 0)),
        compiler_params=_params(("arbitrary",), blocks, 2 * _nbytes((tm, d), F32)),
        name="prenorm",
    )(x2, gain.reshape(1, d), mod_l, mod_l)


def _mm_kernel(a_ref, b_ref, o_ref):
    o_ref[...] = _dot(a_ref[...], b_ref[...]).astype(o_ref.dtype)


def _matmul(a, b, out_dtype):
    m, k = a.shape
    n = b.shape[1]
    tm, tn = _tile(m, TM_MM), _tile(n, TN_MM)
    blocks = _nbytes((tm, k), BF16) + _nbytes((k, tn), BF16) + _nbytes((tm, tn), out_dtype)
    return pl.pallas_call(
        _mm_kernel,
        out_shape=jax.ShapeDtypeStruct((m, n), out_dtype),
        grid=(m // tm, n // tn),
        in_specs=[pl.BlockSpec((tm, k), lambda i, j: (i, 0)),
                  pl.BlockSpec((k, tn), lambda i, j: (0, j))],
        out_specs=pl.BlockSpec((tm, tn), lambda i, j: (i, j)),
        compiler_params=_params(("arbitrary", "arbitrary"), blocks, 3 * _nbytes((tm, tn), F32)),
        name="in_proj",
    )(a, b)


def _dwconv3(s, w):
    n = s.shape[0]
    row = lax.broadcasted_iota(jnp.int32, s.shape, 0)
    prev = jnp.where(row == 0, 0.0, pltpu.roll(s, 1, 0))
    nxt = jnp.where(row == n - 1, 0.0, pltpu.roll(s, n - 1, 0))
    return prev * w[0:1, :] + s * w[1:2, :] + nxt * w[2:3, :]


def _sconv_kernel(b_ref, c_ref, x_ref, w_ref, o_ref):
    s = c_ref[...].astype(F32) * x_ref[...].astype(F32)
    o_ref[...] = (b_ref[...].astype(F32) * _dwconv3(s, w_ref[...])).astype(BF16)


def _short_conv(proj3, conv_w, col0):
    bsz, length, _ = proj3.shape
    width = conv_w.shape[1]
    tc = _tile(width, TC_CONV)
    nb = width // tc
    base = col0 // tc
    blocks = 3 * _nbytes((length, tc), proj3.dtype) + _nbytes((3, tc), F32) + _nbytes((length, tc), BF16)

    def col(g):
        return pl.BlockSpec((None, length, tc), lambda b, j: (b, 0, base + g * nb + j))

    return pl.pallas_call(
        _sconv_kernel,
        out_shape=jax.ShapeDtypeStruct((bsz, length, width), BF16),
        grid=(bsz, nb),
        in_specs=[col(0), col(1), col(2), pl.BlockSpec((3, tc), lambda b, j: (0, j))],
        out_specs=pl.BlockSpec((None, length, tc), lambda b, j: (b, 0, j)),
        compiler_params=_params(("arbitrary", "arbitrary"), blocks, 6 * _nbytes((length, tc), F32)),
        name="short_conv",
    )(proj3, proj3, proj3, conv_w)


def _hyena_pre_kernel(p0_ref, p1_ref, p2_ref, w0_ref, w1_ref, w2_ref, x0_ref, z_ref):
    x0_ref[...] = _dwconv3(p0_ref[...].astype(F32), w0_ref[...]).astype(BF16)
    x1 = _dwconv3(p1_ref[...].astype(F32), w1_ref[...])
    v = _dwconv3(p2_ref[...].astype(F32), w2_ref[...])
    z_ref[...] = (x1 * v).astype(BF16)


def _hyena_pre(proj3, conv_w, col0):
    bsz, length, _ = proj3.shape
    width = conv_w.shape[1] // 3
    tc = _tile(width, TC_CONV)
    nb = width // tc
    base = col0 // tc
    blocks = (3 * _nbytes((length, tc), proj3.dtype) + 3 * _nbytes((3, tc), F32)
              + 2 * _nbytes((length, tc), BF16))

    def col(g):
        return pl.BlockSpec((None, length, tc), lambda b, j: (b, 0, base + g * nb + j))

    def wcol(g):
        return pl.BlockSpec((3, tc), lambda b, j: (0, g * nb + j))

    out = jax.ShapeDtypeStruct((bsz, length, width), BF16)
    ospec = pl.BlockSpec((None, length, tc), lambda b, j: (b, 0, j))
    return pl.pallas_call(
        _hyena_pre_kernel,
        out_shape=(out, out),
        grid=(bsz, nb),
        in_specs=[col(0), col(1), col(2), wcol(0), wcol(1), wcol(2)],
        out_specs=(ospec, ospec),
        compiler_params=_params(("arbitrary", "arbitrary"), blocks, 8 * _nbytes((length, tc), F32)),
        name="hyena_pre",
    )(proj3, proj3, proj3, conv_w, conv_w, conv_w)


def _softmax_pv(parts):
    m = functools.reduce(jnp.maximum, [jnp.max(s, axis=-1, keepdims=True) for s, _ in parts])
    ps = [jnp.exp(s - m) for s, _ in parts]
    denom = functools.reduce(jnp.add, [jnp.sum(p, axis=-1, keepdims=True) for p in ps])
    acc = functools.reduce(jnp.add, [_dot(p.astype(BF16), v) for p, (_, v) in zip(ps, parts)])
    return acc / denom


def _ctx_attn_kernel(q_ref, k_ref, v_ref, o_ref, *, heads, head_dim):
    scale = head_dim ** -0.5
    for h in range(heads):
        sl = slice(h * head_dim, (h + 1) * head_dim)
        q = q_ref[:, sl].astype(BF16)
        k = k_ref[:, sl].astype(BF16)
        v = v_ref[:, sl].astype(BF16)
        o_ref[:, sl] = _softmax_pv([(_dot_t(q, k) * scale, v)]).astype(BF16)


def _context_attention(proj3, col0, n_heads, head_dim):
    bsz, length, _ = proj3.shape
    hp = _heads_per_step(n_heads, head_dim, col0, HEADS_PER_STEP_CTX)
    bw = hp * head_dim
    nb = n_heads // hp
    base = col0 // bw
    blocks = 3 * _nbytes((length, bw), proj3.dtype) + _nbytes((length, bw), BF16)

    def col(g):
        return pl.BlockSpec((None, length, bw), lambda b, j: (b, 0, base + g * nb + j))

    return pl.pallas_call(
        functools.partial(_ctx_attn_kernel, heads=hp, head_dim=head_dim),
        out_shape=jax.ShapeDtypeStruct((bsz, length, n_heads * head_dim), BF16),
        grid=(bsz, nb),
        in_specs=[col(0), col(1), col(2)],
        out_specs=pl.BlockSpec((None, length, bw), lambda b, j: (b, 0, j)),
        compiler_params=_params(("arbitrary", "arbitrary"), blocks, 8 * _nbytes((length, length), F32)),
        name="context_attention",
    )(proj3, proj3, proj3)


def _heads_per_step(n_heads, head_dim, col0, cap):
    return max(h for h in range(1, cap + 1) if n_heads % h == 0 and col0 % (h * head_dim) == 0)


def _nat_kernel(q_ref, k_ref, v_ref, kc_ref, vc_ref, bias_ref, o_ref, mc_ref, lc_ref, oc_ref,
                *, rows, kh, heads, head_dim):
    scale = head_dim ** -0.5
    length = rows * GRID_W
    past = kc_ref.shape[0]
    cch = _tile(length, NAT_CTX_CHUNK)
    n_cch = length // cch
    c_unroll = NAT_UNROLL if n_cch % NAT_UNROLL == 0 else 1
    span = NAT_BLOCK_ROWS + kh
    n_blk = rows // NAT_BLOCK_ROWS
    b_unroll = NAT_UNROLL if n_blk % NAT_UNROLL == 0 else 1
    qn, kn = NAT_BLOCK_ROWS * GRID_W, span * GRID_W
    lanes = [slice(h * head_dim, (h + 1) * head_dim) for h in range(heads)]

    def block_diag(parts):
        zero = jnp.zeros_like(parts[0])
        return jnp.concatenate([jnp.concatenate([p if j == i else zero for j in range(heads)], axis=-1)
                                for i, p in enumerate(parts)], axis=0)

    kc = block_diag([kc_ref[:, ls].astype(BF16) for ls in lanes])
    vc = block_diag([vc_ref[:, ls].astype(BF16) for ls in lanes])

    def ctx_group(g, carry):
        rws = [pl.ds(pl.multiple_of((g * c_unroll + u) * cch, cch), cch) for u in range(c_unroll)]
        scores = [_dot_t(q_ref[rw, :], kc) * scale for rw in rws]
        probs = []
        for rw, s in zip(rws, scores):
            per_head = []
            for h in range(heads):
                sh = s[:, h * past:(h + 1) * past]
                m = jnp.max(sh, axis=-1, keepdims=True)
                p = jnp.exp(sh - m)
                mc_ref[h, rw, :] = m
                lc_ref[h, rw, :] = jnp.sum(p, axis=-1, keepdims=True)
                per_head.append(p.astype(BF16))
            probs.append(jnp.concatenate(per_head, axis=-1))
        for rw, p in zip(rws, probs):
            oc_ref[rw, :] = _dot(p, vc)
        return carry

    lax.fori_loop(0, n_cch // c_unroll, ctx_group, 0)

    def block_group(g, carry):
        qrows, wins, kinds = [], [], []
        for u in range(b_unroll):
            blk = g * b_unroll + u
            r0 = blk * NAT_BLOCK_ROWS
            start = jnp.clip(r0 - kh // 2, 0, rows - span)
            qrows.append(pl.ds(pl.multiple_of(r0 * GRID_W, qn), qn))
            wins.append(pl.ds(pl.multiple_of(start * GRID_W, GRID_W), kn))
            kinds.append(jnp.where(blk == 0, 0, jnp.where(blk == n_blk - 1, 2, 1)))
        scores = [_dot_t(q_ref[qr, :], block_diag([k_ref[w, ls] for ls in lanes])) * scale
                  for qr, w in zip(qrows, wins)]
        probs, wcs, denoms = [], [], []
        for qr, s, kd in zip(qrows, scores, kinds):
            per_head, wc_h, denom_h = [], [], []
            for h in range(heads):
                sh = s[:, h * kn:(h + 1) * kn] + bias_ref[h, kd]
                mc = mc_ref[h, qr, :]
                m = jnp.maximum(jnp.max(sh, axis=-1, keepdims=True), mc)
                p = jnp.exp(sh - m)
                wc = jnp.exp(mc - m)
                per_head.append(p.astype(BF16))
                wc_h.append(wc)
                denom_h.append(jnp.sum(p, axis=-1, keepdims=True) + lc_ref[h, qr, :] * wc)
            probs.append(jnp.concatenate(per_head, axis=-1))
            wcs.append(wc_h)
            denoms.append(denom_h)
        accs = [_dot(p, block_diag([v_ref[w, ls] for ls in lanes])) for p, w in zip(probs, wins)]
        for qr, acc, wc_h, denom_h in zip(qrows, accs, wcs, denoms):
            for h, ls in enumerate(lanes):
                o_ref[qr, ls] = ((acc[:, ls] + oc_ref[qr, ls] * wc_h[h]) / denom_h[h]).astype(BF16)
        return carry

    lax.fori_loop(0, n_blk // b_unroll, block_group, 0)


def _nat_bias_strips(rpb, kh, kw):
    n_heads = rpb.shape[0]
    qc = np.arange(GRID_W)[:, None]
    kc = np.arange(GRID_W)[None, :]
    cstart = np.clip(qc - kw // 2, 0, GRID_W - kw)
    ok = (kc >= cstart) & (kc < cstart + kw)
    padded = jnp.pad(rpb, ((0, 0), (0, 0), (GRID_W - kw, GRID_W - kw)))
    toep = jnp.stack([padded[:, :, GRID_W - 1 - q:2 * GRID_W - 1 - q] for q in range(GRID_W)], axis=2)
    toep = jnp.where(ok[None, None], toep, -jnp.inf)
    return jnp.stack([jnp.transpose(toep[:, o:o + kh], (0, 2, 1, 3)).reshape(n_heads, GRID_W, kh * GRID_W)
                      for o in range(kh)], axis=1)


def _nat_block_tiles(strips, rows):
    n_heads, kh = strips.shape[:2]
    span = NAT_BLOCK_ROWS + kh
    n_blk = rows // NAT_BLOCK_ROWS
    assert rows % NAT_BLOCK_ROWS == 0 and rows >= span and NAT_BLOCK_ROWS >= kh // 2

    def ninf(width):
        return jnp.full((n_heads, GRID_W, width * GRID_W), -jnp.inf, F32)

    kinds = []
    for blk in (0, min(1, n_blk - 1), n_blk - 1):
        r0 = blk * NAT_BLOCK_ROWS
        start = min(max(r0 - kh // 2, 0), rows - span)
        slabs = []
        for r in range(r0, r0 + NAT_BLOCK_ROWS):
            rs = min(max(r - kh // 2, 0), rows - kh)
            lead = rs - start
            slabs.append(jnp.concatenate([ninf(lead), strips[:, rs - r + kh - 1], ninf(span - kh - lead)], axis=-1))
        kinds.append(jnp.concatenate(slabs, axis=1))
    return jnp.stack(kinds, axis=1)


def _neighbourhood_attention(proj3, col0, cache_k4, cache_v4, layer, tiles, kh, n_heads, head_dim):
    bsz, length, _ = proj3.shape
    rows = length // GRID_W
    past = cache_k4.shape[2]
    hp = _heads_per_step(n_heads, head_dim, col0, HEADS_PER_STEP_NAT)
    bw = hp * head_dim
    nb = n_heads // hp
    base = col0 // bw
    blocks = (3 * _nbytes((length, bw), proj3.dtype) + 2 * _nbytes((past, bw), F32)
              + _nbytes((hp,) + tiles.shape[1:], F32) + _nbytes((length, bw), BF16))
    scratch = [pltpu.VMEM((hp, length, 1), F32), pltpu.VMEM((hp, length, 1), F32), pltpu.VMEM((length, bw), F32)]
    temps = (2 * hp * _nbytes((length, V7X_LANES), F32) + _nbytes((length, bw), F32)
             + 6 * hp * NAT_UNROLL * _nbytes((_tile(length, NAT_CTX_CHUNK), past), F32)
             + 6 * hp * NAT_UNROLL * _nbytes(tiles.shape[2:], F32))

    def col(g):
        return pl.BlockSpec((None, length, bw), lambda b, j: (b, 0, base + g * nb + j))

    cache_spec = pl.BlockSpec((None, None, past, bw), lambda b, j: (b, layer, 0, j))
    return pl.pallas_call(
        functools.partial(_nat_kernel, rows=rows, kh=kh, heads=hp, head_dim=head_dim),
        out_shape=jax.ShapeDtypeStruct((bsz, length, n_heads * head_dim), BF16),
        grid=(bsz, nb),
        in_specs=[col(0), col(1), col(2), cache_spec, cache_spec,
                  pl.BlockSpec((hp,) + tiles.shape[1:], lambda b, j: (j, 0, 0, 0))],
        out_specs=pl.BlockSpec((None, length, bw), lambda b, j: (b, 0, j)),
        scratch_shapes=scratch,
        compiler_params=_params(("arbitrary", "arbitrary"), blocks, temps),
        name="neighbourhood_attention",
    )(proj3, proj3, proj3, cache_k4, cache_v4, tiles)


def _dft_matrices(length):
    n = 2 * length
    split = min(DFT_SPLIT, length)
    s = np.arange(length, dtype=np.int64)
    pa = ((np.arange(length // split, dtype=np.int64)[:, None] * split * s[None, :]) % n).astype(np.int32)
    pb = ((np.arange(split, dtype=np.int64)[:, None] * s[None, :]) % n).astype(np.int32)
    ta = jnp.asarray(pa).astype(F32) * (2.0 * math.pi / n)
    tb = jnp.asarray(pb).astype(F32) * (2.0 * math.pi / n)
    ca, sa, cb, sb = jnp.cos(ta), jnp.sin(ta), jnp.cos(tb), jnp.sin(tb)
    cosm = (ca[:, None, :] * cb[None, :, :] - sa[:, None, :] * sb[None, :, :]).reshape(length, length)
    msin = -(sa[:, None, :] * cb[None, :, :] + ca[:, None, :] * sb[None, :, :]).reshape(length, length)
    k_idx = lax.broadcasted_iota(jnp.int32, (length, length), 0)
    s_idx = lax.broadcasted_iota(jnp.int32, (length, length), 1)
    sinm = jnp.where(k_idx == 0, jnp.where(s_idx % 2 == 0, 1.0, -1.0), msin)
    sinm_t = jnp.where(s_idx == 0, jnp.where(k_idx % 2 == 0, 1.0, -1.0), msin)
    return cosm.astype(BF16), sinm.astype(BF16), sinm_t.astype(BF16)


def _filter_kernel(bands_ref, w1_ref, b1_ref, f_ref, w2_ref, b2_ref, w3_ref, dl_ref, o_ref,
                   *, length, tl, emb_bands, width):
    hi = lax.Precision.HIGHEST
    pos = (lax.broadcasted_iota(jnp.int32, (tl, V7X_LANES), 0) + pl.program_id(0) * tl).astype(F32)
    lane = lax.broadcasted_iota(jnp.int32, (tl, V7X_LANES), 1)
    t = pos * (1.0 / (length - 1))
    ang = ((2.0 * math.pi / length) * pos) * bands_ref[...]
    feat = jnp.where(lane == 0, t,
                     jnp.where(lane <= emb_bands, jnp.cos(ang),
                               jnp.where(lane <= 2 * emb_bands, -jnp.sin(ang), 0.0)))
    h = jnp.sin(f_ref[0:1, :] * (jnp.dot(feat, w1_ref[...], precision=hi, preferred_element_type=F32)
                                 + b1_ref[...]))
    h = jnp.sin(f_ref[1:2, :] * (jnp.dot(h, w2_ref[...], precision=hi, preferred_element_type=F32)
                                 + b2_ref[...]))
    h = jnp.dot(h, w3_ref[...], precision=hi, preferred_element_type=F32)
    decay = jnp.exp(-(t[:, 0:1]) * dl_ref[...])
    col = lax.broadcasted_iota(jnp.int32, h.shape, 1)
    first = (pos[:, 0:1] == 0.0) & (col >= width)
    o_ref[...] = jnp.where(first, 0.0, h * decay).astype(BF16)


def _hyena_filters(length, w1, b1, freq, w2, b2, w3):
    emb, fo = w1.shape
    width = w3.shape[1] // 2
    emb_bands = (emb - 1) // 2
    tl = _tile(length, TM_DFT)
    bands = jnp.linspace(1e-4, emb_bands - 1, emb_bands, dtype=F32)
    bands_row = jnp.zeros((1, V7X_LANES), F32).at[0, 1:1 + emb_bands].set(bands)
    bands_row = bands_row.at[0, 1 + emb_bands:1 + 2 * emb_bands].set(bands)
    w1p = jnp.zeros((V7X_LANES, fo), F32).at[:emb].set(w1)
    deltas = jnp.abs(jnp.linspace(math.log(HY_TARGET) / HY_SLOW, math.log(HY_TARGET) / HY_FAST, width, dtype=F32))
    dl = jnp.concatenate([deltas, deltas]).reshape(1, 2 * width)
    full = lambda shape: pl.BlockSpec(shape, lambda i: (0,) * len(shape))
    blocks = _nbytes((tl, 2 * width), BF16) + _nbytes((fo + 2, 2 * width), F32) + _nbytes((V7X_LANES + fo, fo), F32)
    return pl.pallas_call(
        functools.partial(_filter_kernel, length=length, tl=tl, emb_bands=emb_bands, width=width),
        out_shape=jax.ShapeDtypeStruct((length, 2 * width), BF16),
        grid=(length // tl,),
        in_specs=[full((1, V7X_LANES)), full((V7X_LANES, fo)), full((1, fo)), full((2, fo)),
                  full((fo, fo)), full((1, fo)), full((fo, 2 * width)), full((1, 2 * width))],
        out_specs=pl.BlockSpec((tl, 2 * width), lambda i: (i, 0)),
        compiler_params=_params(("arbitrary",), blocks, 6 * _nbytes((tl, 2 * width), F32)),
        name="hyena_filters",
    )(bands_row, w1p, b1.reshape(1, fo), freq, w2, b2.reshape(1, fo), w3, dl)


def _spectrum_kernel(c_ref, s_ref, hf_ref, hb_ref, fr_ref, fi_ref, *, tm, norm):
    cm, sm, hf, hb = c_ref[...], s_ref[...], hf_ref[...], hb_ref[...]
    row = lax.broadcasted_iota(jnp.int32, fr_ref.shape, 0) + pl.program_id(0) * tm
    packed = row == 0
    wgt = jnp.where(packed, 0.5 * norm, norm)
    bi = _dot(sm, hb)
    fr_ref[...] = (_dot(cm, hf) + _dot(cm, hb)) * wgt
    fi_ref[...] = (_dot(sm, hf) + jnp.where(packed, bi, -bi)) * wgt


def _filter_spectrum(cosm, sinm, hcat):
    length = cosm.shape[0]
    width = hcat.shape[1] // 2
    tm, tn = _tile(length, TM_DFT), _tile(width, TN_DFT)
    nb = width // tn
    blocks = 2 * _nbytes((tm, length), BF16) + 2 * _nbytes((length, tn), BF16) + 2 * _nbytes((tm, tn), F32)
    out = jax.ShapeDtypeStruct((length, width), F32)
    ospec = pl.BlockSpec((tm, tn), lambda i, j: (i, j))
    return pl.pallas_call(
        functools.partial(_spectrum_kernel, tm=tm, norm=1.0 / length),
        out_shape=(out, out),
        grid=(length // tm, nb),
        in_specs=[pl.BlockSpec((tm, length), lambda i, j: (i, 0)),
                  pl.BlockSpec((tm, length), lambda i, j: (i, 0)),
                  pl.BlockSpec((length, tn), lambda i, j: (0, j)),
                  pl.BlockSpec((length, tn), lambda i, j: (0, nb + j))],
        out_specs=(ospec, ospec),
        compiler_params=_params(("arbitrary", "arbitrary"), blocks, 16 * _nbytes((tm, tn), F32)),
        name="filter_spectrum",
    )(cosm, sinm, hcat, hcat)


def _dft_fwd_kernel(c_ref, s_ref, z_ref, fr_ref, fi_ref, yr_ref, yi_ref, *, tm):
    z = z_ref[...]
    zr, zi = _dot(c_ref[...], z), _dot(s_ref[...], z)
    fr, fi = fr_ref[...], fi_ref[...]
    packed = (lax.broadcasted_iota(jnp.int32, zr.shape, 0) + pl.program_id(0) * tm) == 0
    yr_ref[...] = jnp.where(packed, zr * fr, zr * fr - zi * fi).astype(BF16)
    yi_ref[...] = jnp.where(packed, zi * fi, zr * fi + zi * fr).astype(BF16)


def _dft_forward(cosm, sinm, z, fr, fi):
    bsz, length, width = z.shape
    tm, tn = _tile(length, TM_DFT), _tile(width, TN_DFT)
    blocks = (2 * _nbytes((tm, length), BF16) + _nbytes((length, tn), BF16) + 2 * _nbytes((tm, tn), F32)
              + 2 * _nbytes((tm, tn), BF16))
    out = jax.ShapeDtypeStruct((bsz, length, width), BF16)
    ospec = pl.BlockSpec((None, tm, tn), lambda i, b, j: (b, i, j))
    return pl.pallas_call(
        functools.partial(_dft_fwd_kernel, tm=tm),
        out_shape=(out, out),
        grid=(length // tm, bsz, width // tn),
        in_specs=[pl.BlockSpec((tm, length), lambda i, b, j: (i, 0)),
                  pl.BlockSpec((tm, length), lambda i, b, j: (i, 0)),
                  pl.BlockSpec((None, length, tn), lambda i, b, j: (b, 0, j)),
                  pl.BlockSpec((tm, tn), lambda i, b, j: (i, j)),
                  pl.BlockSpec((tm, tn), lambda i, b, j: (i, j))],
        out_specs=(ospec, ospec),
        compiler_params=_params(("arbitrary",) * 3, blocks, 16 * _nbytes((tm, tn), F32)),
        name="dft_forward",
    )(cosm, sinm, z, fr, fi)


def _dft_inv_kernel(c_ref, st_ref, yr_ref, yi_ref, x0_ref, z_ref, b_ref, o_ref):
    y = _dot(c_ref[...], yr_ref[...]) + _dot(st_ref[...], yi_ref[...])
    y = y + z_ref[...].astype(F32) * b_ref[...]
    o_ref[...] = (x0_ref[...].astype(F32) * y).astype(BF16)


def _dft_inverse(cosm, sinm_t, yr, yi, x0, z, bias):
    bsz, length, width = z.shape
    tm, tn = _tile(length, TM_DFT), _tile(width, TN_DFT)
    blocks = (2 * _nbytes((tm, length), BF16) + 2 * _nbytes((length, tn), BF16)
              + 3 * _nbytes((tm, tn), BF16) + _nbytes((1, tn), F32))
    tile = pl.BlockSpec((None, tm, tn), lambda i, b, j: (b, i, j))
    panel = pl.BlockSpec((None, length, tn), lambda i, b, j: (b, 0, j))
    return pl.pallas_call(
        _dft_inv_kernel,
        out_shape=jax.ShapeDtypeStruct((bsz, length, width), BF16),
        grid=(length // tm, bsz, width // tn),
        in_specs=[pl.BlockSpec((tm, length), lambda i, b, j: (i, 0)),
                  pl.BlockSpec((tm, length), lambda i, b, j: (i, 0)),
                  panel, panel, tile, tile,
                  pl.BlockSpec((1, tn), lambda i, b, j: (0, j))],
        out_specs=tile,
        compiler_params=_params(("arbitrary",) * 3, blocks, 16 * _nbytes((tm, tn), F32)),
        name="dft_inverse",
    )(cosm, sinm_t, yr, yi, x0, z, bias.reshape(1, width))


def _merge_kernel(ya_ref, yb_ref, yc_ref, ga_ref, gb_ref, gc_ref, wa_ref, wb_ref, wc_ref, o_ref):
    m = jax.nn.sigmoid(ga_ref[...].astype(F32)) * _dot(ya_ref[...], wa_ref[...])
    m = m + jax.nn.sigmoid(gb_ref[...].astype(F32)) * _dot(yb_ref[...], wb_ref[...])
    m = m + jax.nn.sigmoid(gc_ref[...].astype(F32)) * _dot(yc_ref[...], wc_ref[...])
    o_ref[...] = m.astype(BF16)


def _merge(ya, yb, yc, proj, gate_col0, wa, wb, wc):
    t = ya.shape[0]
    d = wa.shape[1]
    tm, tn = _tile(t, TM_MM), _tile(d, TN_HALF)
    nb = d // tn
    base = gate_col0 // tn
    ka, kb, kc = ya.shape[1], yb.shape[1], yc.shape[1]
    blocks = (_nbytes((tm, ka + kb + kc), BF16) + 3 * _nbytes((tm, tn), proj.dtype)
              + _nbytes((ka + kb + kc, tn), BF16) + _nbytes((tm, tn), BF16))

    def gate(g):
        return pl.BlockSpec((tm, tn), lambda i, j: (i, base + g * nb + j))

    def panel(k):
        return pl.BlockSpec((tm, k), lambda i, j: (i, 0))

    def wcol(k):
        return pl.BlockSpec((k, tn), lambda i, j: (0, j))

    return pl.pallas_call(
        _merge_kernel,
        out_shape=jax.ShapeDtypeStruct((t, d), BF16),
        grid=(t // tm, nb),
        in_specs=[panel(ka), panel(kb), panel(kc), gate(0), gate(1), gate(2), wcol(ka), wcol(kb), wcol(kc)],
        out_specs=pl.BlockSpec((tm, tn), lambda i, j: (i, j)),
        compiler_params=_params(("arbitrary", "arbitrary"), blocks, 16 * _nbytes((tm, tn), F32)),
        name="gated_merge",
    )(ya, yb, yc, proj, proj, proj, wa, wb, wc)


def _lane_partial_sum(v):
    return functools.reduce(jnp.add, [v[:, t * V7X_LANES:(t + 1) * V7X_LANES]
                                      for t in range(v.shape[1] // V7X_LANES)])


def _mm_epilogue_kernel(a_ref, b_ref, x_ref, gpost_ref, gate_ref, *rest, n_tiles, n_chunks, by_columns,
                        with_next):
    if with_next:
        gnext_ref, sc_ref, sh_ref, xo_ref, h_ref, acc_ref = rest
    else:
        xo_ref, acc_ref = rest
    i = pl.program_id(0)
    k = pl.program_id(1)
    _, n_panels, tm, pw = acc_ref.shape
    d = n_panels * pw
    rc = tm // n_chunks
    ew = _tile(pw, EPI_COL_PIECE)
    chunk = jnp.minimum(k, n_chunks - 1)

    @pl.when((i == 0) & (k == 0))
    def _():
        acc_ref[...] = jnp.zeros_like(acc_ref)

    def accumulate(slot):
        a = a_ref[...]
        if by_columns:
            acc_ref[slot, k] = _dot(a, b_ref[...])
        else:
            for c in range(n_panels):
                part = _dot(a, b_ref[:, c * pw:(c + 1) * pw])
                acc_ref[slot, c] = jnp.where(k == 0, part, acc_ref[slot, c] + part)

    def epilogue(slot):
        rows = pl.ds(pl.multiple_of(chunk * rc, rc), rc)
        pieces = [(c, slice(e * ew, (e + 1) * ew), slice(c * pw + e * ew, c * pw + (e + 1) * ew))
                  for c in range(n_panels) for e in range(pw // ew)]
        ssq = functools.reduce(jnp.add, [_lane_partial_sum(jnp.square(acc_ref[slot, c, rows, ps]))
                                         for c, ps, _ in pieces])
        rstd = lax.rsqrt(jnp.sum(ssq, axis=-1, keepdims=True) / d + RMS_EPS)
        ssq = jnp.zeros((rc, V7X_LANES), F32)
        for c, ps, cs in pieces:
            y = acc_ref[slot, c, rows, ps] * rstd * gpost_ref[:, cs]
            xn = x_ref[:, cs] + gate_ref[:, cs] * y
            xo_ref[:, cs] = xn
            ssq = ssq + _lane_partial_sum(jnp.square(xn))
        if with_next:
            rstd = lax.rsqrt(jnp.sum(ssq, axis=-1, keepdims=True) / d + RMS_EPS)
            for _, _, cs in pieces:
                hn = xo_ref[:, cs] * rstd * gnext_ref[:, cs]
                h_ref[:, cs] = (hn * (1.0 + sc_ref[:, cs]) + sh_ref[:, cs]).astype(BF16)

    @pl.when(i == 0)
    def _():
        accumulate(0)

    inner = (i > 0) & (i < n_tiles)

    @pl.when(inner & (i % 2 == 0))
    def _():
        accumulate(0)
        epilogue(1)

    @pl.when(inner & (i % 2 == 1))
    def _():
        accumulate(1)
        epilogue(0)

    @pl.when(i == n_tiles)
    def _():
        epilogue((n_tiles - 1) % 2)


def _matmul_epilogue(a, b, x2, gain_post, mod_l, gate_idx, rows_per_batch, nxt):
    t, kdim = a.shape
    d = b.shape[1]
    tm, tpb = _row_tiling(mod_l, t, rows_per_batch, TM_EPI)
    n_tiles = t // tm
    by_columns = kdim <= EPI_FULL_DEPTH_MAX
    if by_columns:
        pw = _tile(d, EPI_COL_PIECE)
        steps = d // pw
        a_spec = pl.BlockSpec((tm, kdim), lambda i, k: (jnp.minimum(i, n_tiles - 1), 0))
        b_spec = pl.BlockSpec((kdim, pw), lambda i, k: (0, jnp.where(i == n_tiles, steps - 1, k)))
        blocks = _nbytes((tm, kdim), BF16) + _nbytes((kdim, pw), BF16)
    else:
        fits = [w for w in EPI_K_TILES if kdim % w == 0]
        tk = next((w for w in fits if kdim // w >= EPI_ROW_CHUNKS), fits[-1])
        pw = _tile(d, EPI_COL_CHUNK)
        steps = kdim // tk
        a_spec = pl.BlockSpec((tm, tk), lambda i, k: (jnp.minimum(i, n_tiles - 1),
                                                      jnp.where(i == n_tiles, steps - 1, k)))
        b_spec = pl.BlockSpec((tk, d), lambda i, k: (jnp.where(i == n_tiles, steps - 1, k), 0))
        blocks = _nbytes((tm, tk), BF16) + _nbytes((tk, d), BF16)
    n_chunks = max(n for n in range(1, min(steps, EPI_ROW_CHUNKS) + 1) if tm % (n * 2 * V7X_SUBLANES) == 0)
    rc = tm // n_chunks

    def prev_tile(i):
        return jnp.maximum(i - 1, 0)

    def chunk_index(i, k):
        return jnp.where(i == 0, 0, (i - 1) * n_chunks + jnp.minimum(k, n_chunks - 1))

    def vec(which):
        return pl.BlockSpec((None, None, 1, d), lambda i, k: (prev_tile(i) // tpb, which, 0, 0))

    row = pl.BlockSpec((1, d), lambda i, k: (0, 0))
    chunk_rows = pl.BlockSpec((rc, d), lambda i, k: (chunk_index(i, k), 0))
    in_specs = [a_spec, b_spec, chunk_rows, row, vec(gate_idx)]
    args = [a, b, x2, gain_post.reshape(1, d), mod_l]
    blocks += 2 * _nbytes((rc, d), F32) + 5 * _nbytes((1, d), F32)
    if nxt is None:
        out_shape = jax.ShapeDtypeStruct((t, d), F32)
        out_specs = chunk_rows
    else:
        gain_next, mod_next, sc_idx, sh_idx = nxt
        in_specs += [row, vec(sc_idx), vec(sh_idx)]
        args += [gain_next.reshape(1, d), mod_next, mod_next]
        out_shape = (jax.ShapeDtypeStruct((t, d), F32), jax.ShapeDtypeStruct((t, d), BF16))
        out_specs = (chunk_rows, chunk_rows)
        blocks += _nbytes((rc, d), BF16)
    temps = (_nbytes((2, tm, d), F32) + 3 * _nbytes((tm, pw), F32)
             + 8 * _nbytes((rc, _tile(pw, EPI_COL_PIECE)), F32))
    return pl.pallas_call(
        functools.partial(_mm_epilogue_kernel, n_tiles=n_tiles, n_chunks=n_chunks, by_columns=by_columns,
                          with_next=nxt is not None),
        out_shape=out_shape,
        grid=(n_tiles + 1, steps),
        in_specs=in_specs,
        out_specs=out_specs,
        scratch_shapes=[pltpu.VMEM((2, d // pw, tm, pw), F32)],
        compiler_params=_params(("arbitrary", "arbitrary"), blocks, temps),
        name="matmul_norm_residual",
    )(*args)


def _ffn_up_kernel(h_ref, wg_ref, wu_ref, o_ref):
    h = h_ref[...]
    o_ref[...] = (jax.nn.silu(_dot(h, wg_ref[...])) * _dot(h, wu_ref[...])).astype(BF16)


def _ffn_up(h, wg, wu):
    t, d = h.shape
    n = wg.shape[1]
    tm, tn = _tile(t, TM_MM), _tile(n, TN_HALF)
    blocks = _nbytes((tm, d), BF16) + 2 * _nbytes((d, tn), BF16) + _nbytes((tm, tn), BF16)
    return pl.pallas_call(
        _ffn_up_kernel,
        out_shape=jax.ShapeDtypeStruct((t, n), BF16),
        grid=(t // tm, n // tn),
        in_specs=[pl.BlockSpec((tm, d), lambda i, j: (i, 0)),
                  pl.BlockSpec((d, tn), lambda i, j: (0, j)),
                  pl.BlockSpec((d, tn), lambda i, j: (0, j))],
        out_specs=pl.BlockSpec((tm, tn), lambda i, j: (i, j)),
        compiler_params=_params(("arbitrary", "arbitrary"), blocks, 16 * _nbytes((tm, tn), F32)),
        name="ffn_up",
    )(h, wg, wu)


def _pad_cols(w, mult):
    pad = (-w.shape[-1]) % mult
    return jnp.pad(w, ((0, 0), (0, 0), (0, pad))) if pad else w


def _pad_rows(w, mult):
    pad = (-w.shape[-2]) % mult
    return jnp.pad(w, ((0, 0), (0, pad), (0, 0))) if pad else w


def kernel(x_prompt, x_sample, cache_k, cache_v, c, c_ctx, w_mod, b_mod, norm_gains, w_in, conv_a, w_up_a, na_rpb, w_up_b, conv_c, filt_w1, filt_b1, filt_freq, filt_w2, filt_b2, filt_w3, hyena_bias, w_up_c, w_out, w_ffn_gate, w_ffn_up, w_ffn_down):
    depth, d, _ = w_mod.shape
    n_heads, head_dim = cache_k.shape[3], cache_k.shape[4]
    sc_w = conv_a.shape[-1]
    na_w = n_heads * head_dim
    hy_w = hyena_bias.shape[-1]
    kh_full = (na_rpb.shape[2] + 1) // 2
    kw = (na_rpb.shape[3] + 1) // 2
    col_na = 3 * sc_w
    col_hy = col_na + 3 * na_w
    col_gate = col_hy + 3 * hy_w
    assert x_sample.shape[1] % GRID_W == 0 and x_sample.shape[1] // GRID_W >= kh_full
    assert 1 + c.shape[0] <= MOD_ROWS_PAD

    w_in_b = w_in.astype(BF16)
    w_up_a_b, w_up_b_b, w_up_c_b = w_up_a.astype(BF16), w_up_b.astype(BF16), w_up_c.astype(BF16)
    w_out_b = w_out.astype(BF16)
    w_gate_b = _pad_cols(w_ffn_gate.astype(BF16), FFN_PAD)
    w_upf_b = _pad_cols(w_ffn_up.astype(BF16), FFN_PAD)
    w_down_b = _pad_rows(w_ffn_down.astype(BF16), FFN_PAD)

    c_rows = jnp.zeros((MOD_ROWS_PAD, d), F32).at[0].set(c_ctx).at[1:1 + c.shape[0]].set(c)
    mod = _modulation(c_rows, w_mod, b_mod).reshape(depth, MOD_ROWS_PAD, N_MOD, 1, d)
    cache_k4 = cache_k.reshape(cache_k.shape[:3] + (na_w,))
    cache_v4 = cache_v.reshape(cache_v.shape[:3] + (na_w,))

    def run_group(x3, mod_g, latent):
        bsz, length, _ = x3.shape
        proj_dtype = BF16 if latent else F32
        cosm, sinm, sinm_t = _dft_matrices(length)
        x2 = x3.reshape(bsz * length, d)
        h = _prenorm(x2, norm_gains[0, 0], mod_g[0], length)
        ks, vs = [], []
        for l in range(depth):
            proj = _matmul(h, w_in_b[l], proj_dtype)
            proj3 = proj.reshape(bsz, length, proj.shape[1])
            y_sc = _short_conv(proj3, conv_a[l], 0)
            if latent:
                tiles = _nat_block_tiles(_nat_bias_strips(na_rpb[l], kh_full, kw), length // GRID_W)
                y_na = _neighbourhood_attention(proj3, col_na, cache_k4, cache_v4, l, tiles, kh_full,
                                                n_heads, head_dim)
            else:
                y_na = _context_attention(proj3, col_na, n_heads, head_dim)
                ks.append(proj3[:, :, col_na + na_w:col_na + 2 * na_w].reshape(bsz, length, n_heads, head_dim))
                vs.append(proj3[:, :, col_na + 2 * na_w:col_hy].reshape(bsz, length, n_heads, head_dim))
            x0, z = _hyena_pre(proj3, conv_c[l], col_hy)
            hcat = _hyena_filters(length, filt_w1[l], filt_b1[l], filt_freq[l], filt_w2[l], filt_b2[l], filt_w3[l])
            fr, fi = _filter_spectrum(cosm, sinm, hcat)
            yr, yi = _dft_forward(cosm, sinm, z, fr, fi)
            y_hy = _dft_inverse(cosm, sinm_t, yr, yi, x0, z, hyena_bias[l])
            merged = _merge(y_sc.reshape(-1, sc_w), y_na.reshape(-1, na_w), y_hy.reshape(-1, hy_w),
                            proj, col_gate, w_up_a_b[l], w_up_b_b[l], w_up_c_b[l])
            x2, h2 = _matmul_epilogue(merged, w_out_b[l], x2, norm_gains[l, 1], mod_g[l], 2, length,
                                      (norm_gains[l, 2], mod_g[l], 4, 3))
            hidden = _ffn_up(h2, w_gate_b[l], w_upf_b[l])
            if l + 1 < depth:
                x2, h = _matmul_epilogue(hidden, w_down_b[l], x2, norm_gains[l, 3], mod_g[l], 5, length,
                                         (norm_gains[l + 1, 0], mod_g[l + 1], 1, 0))
            else:
                x2 = _matmul_epilogue(hidden, w_down_b[l], x2, norm_gains[l, 3], mod_g[l], 5, length, None)
        return x2.reshape(bsz, length, d), ks, vs

    y_prompt, ks, vs = run_group(x_prompt, mod[:, 0:1], latent=False)
    y_sample, _, _ = run_group(x_sample, mod[:, 1:1 + c.shape[0]], latent=True)
    return (y_prompt, y_sample, jnp.stack(ks, axis=1), jnp.stack(vs, axis=1))
```

```python
import functools
import itertools
import math

import numpy as np
import jax
import jax.numpy as jnp
from jax import lax
from jax.experimental import pallas as pl
from jax.experimental.pallas import tpu as pltpu

F32 = jnp.float32
BF16 = jnp.bfloat16

GRID_W = 64
HY_FAST = 0.3
HY_SLOW = 1.5
HY_TARGET = 1e-2
RMS_EPS = 1e-6
N_MOD = 6
N_BRANCH = 3

V7X_VMEM_BYTES = 64 * 1024 * 1024
V7X_VMEM_RESERVED_BYTES = 6 * 1024 * 1024
V7X_LANES = 128
V7X_SUBLANES = 8
MOD_ROWS_PAD = 16

TM_MM = 1024
TN_MM = 1024
TN_HALF = 512
TM_EPI = 512
EPI_K_TILES = (1024, 512, 256, 128)
EPI_FULL_DEPTH_MAX = 4096
EPI_ROW_CHUNKS = 8
EPI_COL_CHUNK = 1024
EPI_COL_PIECE = 512
TM_DFT = 512
TN_DFT = 512
CONV_TILE_ELEMS = 512 * 1024
FFN_PAD = 1024
HEADS_PER_STEP_CTX = 8
HEADS_PER_STEP_NAT = 2
NAT_CTX_CHUNK = 512
NAT_BLOCK_ROWS = 4
NAT_UNROLL = 4
DFT_SPLIT = 64


def _tile(dim, pref):
    return pref if dim % pref == 0 else dim


def _nbytes(shape, dtype):
    return int(np.prod(shape)) * jnp.dtype(dtype).itemsize


def _params(semantics, block_bytes, temp_bytes=0):
    need = 2 * block_bytes + temp_bytes
    limit = min(V7X_VMEM_BYTES - V7X_VMEM_RESERVED_BYTES, max(need, 16 * 1024 * 1024))
    return pltpu.CompilerParams(dimension_semantics=semantics, vmem_limit_bytes=limit)


def _dot(a, b):
    return jnp.dot(a, b, preferred_element_type=F32)


def _dot_t(a, b):
    return lax.dot_general(a, b, (((1,), (1,)), ((), ())), preferred_element_type=F32)


def _rms(x, gain):
    return x * lax.rsqrt(jnp.mean(x * x, axis=-1, keepdims=True) + RMS_EPS) * gain


def _mod_kernel(c_ref, w_ref, b_ref, o_ref):
    a = jax.nn.silu(c_ref[...]).astype(BF16)
    o_ref[...] = _dot(a, w_ref[...].astype(BF16)) + b_ref[...]


def _modulation(c_rows, w_mod, b_mod):
    depth, d, n = w_mod.shape
    tn = _tile(n, TN_HALF)
    blocks = _nbytes((MOD_ROWS_PAD, d), F32) + _nbytes((d, tn), F32) + _nbytes((MOD_ROWS_PAD + 1, tn), F32)
    return pl.pallas_call(
        _mod_kernel,
        out_shape=jax.ShapeDtypeStruct((depth, MOD_ROWS_PAD, n), F32),
        grid=(depth, n // tn),
        in_specs=[pl.BlockSpec((MOD_ROWS_PAD, d), lambda l, j: (0, 0)),
                  pl.BlockSpec((None, d, tn), lambda l, j: (l, 0, j)),
                  pl.BlockSpec((None, 1, tn), lambda l, j: (l, 0, j))],
        out_specs=pl.BlockSpec((None, MOD_ROWS_PAD, tn), lambda l, j: (l, 0, j)),
        compiler_params=_params(("arbitrary", "arbitrary"), blocks, _nbytes((d, tn), BF16)),
        name="modulation",
    )(c_rows, w_mod, b_mod.reshape(depth, 1, n))


def _row_tiling(mod_l, total_rows, rows_per_batch, pref):
    shared = mod_l.shape[0] == 1
    tm = _tile(total_rows if shared else rows_per_batch, pref)
    return tm, (total_rows if shared else rows_per_batch) // tm


def _vec_spec(d, which, tiles_per_batch):
    return pl.BlockSpec((None, None, 1, d), lambda i: (i // tiles_per_batch, which, 0, 0))


def _prenorm_kernel(x_ref, g_ref, sc_ref, sh_ref, h_ref):
    y = _rms(x_ref[...], g_ref[...])
    h_ref[...] = (y * (1.0 + sc_ref[...]) + sh_ref[...]).astype(BF16)


def _prenorm(x2, gain, mod_l, rows_per_batch):
    t, d = x2.shape
    tm, tpb = _row_tiling(mod_l, t, rows_per_batch, TM_EPI)
    blocks = _nbytes((tm, d), F32) + _nbytes((tm, d), BF16) + 3 * _nbytes((1, d), F32)
    return pl.pallas_call(
        _prenorm_kernel,
        out_shape=jax.ShapeDtypeStruct((t, d), BF16),
        grid=(t // tm,),
        in_specs=[pl.BlockSpec((tm, d), lambda i: (i, 0)),
                  pl.BlockSpec((1, d), lambda i: (0, 0)),
                  _vec_spec(d, 1, tpb),
                  _vec_spec(d, 0, tpb)],
        out_specs=pl.BlockSpec((tm, d), lambda i: (i, 0)),
        compiler_params=_params(("arbitrary",), blocks, 2 * _nbytes((tm, d), F32)),
        name="prenorm",
    )(x2, gain.reshape(1, d), mod_l, mod_l)


def _mm_kernel(a_ref, b_ref, o_ref):
    o_ref[...] = _dot(a_ref[...], b_ref[...]).astype(o_ref.dtype)


def _matmul(a, b, out_dtype):
    m, k = a.shape
    n = b.shape[1]
    tm, tn = _tile(m, TM_MM), _tile(n, TN_MM)
    blocks = _nbytes((tm, k), BF16) + _nbytes((k, tn), BF16) + _nbytes((tm, tn), out_dtype)
    return pl.pallas_call(
        _mm_kernel,
        out_shape=jax.ShapeDtypeStruct((m, n), out_dtype),
        grid=(m // tm, n // tn),
        in_specs=[pl.BlockSpec((tm, k), lambda i, j: (i, 0)),
                  pl.BlockSpec((k, tn), lambda i, j: (0, j))],
        out_specs=pl.BlockSpec((tm, tn), lambda i, j: (i, j)),
        compiler_params=_params(("arbitrary", "arbitrary"), blocks, 3 * _nbytes((tm, tn), F32)),
        name="in_proj",
    )(a, b)


def _dwconv3(s, w):
    n = s.shape[0]
    row = lax.broadcasted_iota(jnp.int32, s.shape, 0)
    prev = jnp.where(row == 0, 0.0, pltpu.roll(s, 1, 0))
    nxt = jnp.where(row == n - 1, 0.0, pltpu.roll(s, n - 1, 0))
    return prev * w[0:1, :] + s * w[1:2, :] + nxt * w[2:3, :]


def _sconv_kernel(b_ref, c_ref, x_ref, w_ref, o_ref):
    s = c_ref[...].astype(F32) * x_ref[...].astype(F32)
    o_ref[...] = (b_ref[...].astype(F32) * _dwconv3(s, w_ref[...])).astype(BF16)


def _short_conv(proj3, conv_w, col0):
    bsz, length, _ = proj3.shape
    width = conv_w.shape[1]
    tc = _tile(width, max(V7X_LANES, CONV_TILE_ELEMS // length))
    nb = width // tc
    base = col0 // tc
    blocks = 3 * _nbytes((length, tc), proj3.dtype) + _nbytes((3, tc), F32) + _nbytes((length, tc), BF16)

    def col(g):
        return pl.BlockSpec((None, length, tc), lambda b, j: (b, 0, base + g * nb + j))

    return pl.pallas_call(
        _sconv_kernel,
        out_shape=jax.ShapeDtypeStruct((bsz, length, width), BF16),
        grid=(bsz, nb),
        in_specs=[col(0), col(1), col(2), pl.BlockSpec((3, tc), lambda b, j: (0, j))],
        out_specs=pl.BlockSpec((None, length, tc), lambda b, j: (b, 0, j)),
        compiler_params=_params(("arbitrary", "arbitrary"), blocks, 6 * _nbytes((length, tc), F32)),
        name="short_conv",
    )(proj3, proj3, proj3, conv_w)


def _hyena_pre_kernel(p0_ref, p1_ref, p2_ref, w0_ref, w1_ref, w2_ref, x0_ref, z_ref):
    x0_ref[...] = _dwconv3(p0_ref[...].astype(F32), w0_ref[...]).astype(BF16)
    x1 = _dwconv3(p1_ref[...].astype(F32), w1_ref[...])
    v = _dwconv3(p2_ref[...].astype(F32), w2_ref[...])
    z_ref[...] = (x1 * v).astype(BF16)


def _hyena_pre(proj3, conv_w, col0):
    bsz, length, _ = proj3.shape
    width = conv_w.shape[1] // 3
    tc = _tile(width, max(V7X_LANES, CONV_TILE_ELEMS // length))
    nb = width // tc
    base = col0 // tc
    blocks = (3 * _nbytes((length, tc), proj3.dtype) + 3 * _nbytes((3, tc), F32)
              + 2 * _nbytes((length, tc), BF16))

    def col(g):
        return pl.BlockSpec((None, length, tc), lambda b, j: (b, 0, base + g * nb + j))

    def wcol(g):
        return pl.BlockSpec((3, tc), lambda b, j: (0, g * nb + j))

    out = jax.ShapeDtypeStruct((bsz, length, width), BF16)
    ospec = pl.BlockSpec((None, length, tc), lambda b, j: (b, 0, j))
    return pl.pallas_call(
        _hyena_pre_kernel,
        out_shape=(out, out),
        grid=(bsz, nb),
        in_specs=[col(0), col(1), col(2), wcol(0), wcol(1), wcol(2)],
        out_specs=(ospec, ospec),
        compiler_params=_params(("arbitrary", "arbitrary"), blocks, 8 * _nbytes((length, tc), F32)),
        name="hyena_pre",
    )(proj3, proj3, proj3, conv_w, conv_w, conv_w)


def _softmax_pv(parts):
    m = functools.reduce(jnp.maximum, [jnp.max(s, axis=-1, keepdims=True) for s, _ in parts])
    ps = [jnp.exp(s - m) for s, _ in parts]
    denom = functools.reduce(jnp.add, [jnp.sum(p, axis=-1, keepdims=True) for p in ps])
    acc = functools.reduce(jnp.add, [_dot(p.astype(BF16), v) for p, (_, v) in zip(ps, parts)])
    return acc / denom


def _ctx_attn_kernel(q_ref, k_ref, v_ref, o_ref, *, heads, head_dim):
    scale = head_dim ** -0.5
    for h in range(heads):
        sl = slice(h * head_dim, (h + 1) * head_dim)
        q = q_ref[:, sl].astype(BF16)
        k = k_ref[:, sl].astype(BF16)
        v = v_ref[:, sl].astype(BF16)
        o_ref[:, sl] = _softmax_pv([(_dot_t(q, k) * scale, v)]).astype(BF16)


def _context_attention(proj3, col0, n_heads, head_dim):
    bsz, length, _ = proj3.shape
    hp = _heads_per_step(n_heads, head_dim, col0, HEADS_PER_STEP_CTX)
    bw = hp * head_dim
    nb = n_heads // hp
    base = col0 // bw
    blocks = 3 * _nbytes((length, bw), proj3.dtype) + _nbytes((length, bw), BF16)

    def col(g):
        return pl.BlockSpec((None, length, bw), lambda b, j: (b, 0, base + g * nb + j))

    return pl.pallas_call(
        functools.partial(_ctx_attn_kernel, heads=hp, head_dim=head_dim),
        out_shape=jax.ShapeDtypeStruct((bsz, length, n_heads * head_dim), BF16),
        grid=(bsz, nb),
        in_specs=[col(0), col(1), col(2)],
        out_specs=pl.BlockSpec((None, length, bw), lambda b, j: (b, 0, j)),
        compiler_params=_params(("arbitrary", "arbitrary"), blocks, 8 * _nbytes((length, length), F32)),
        name="context_attention",
    )(proj3, proj3, proj3)


def _heads_per_step(n_heads, head_dim, col0, cap):
    return max(h for h in range(1, cap + 1) if n_heads % h == 0 and col0 % (h * head_dim) == 0)


def _nat_kernel(q_ref, k_ref, v_ref, kc_ref, vc_ref, bias_ref, o_ref, mc_ref, lc_ref, oc_ref,
                *, rows, kh, heads, head_dim):
    scale = head_dim ** -0.5
    length = rows * GRID_W
    past = kc_ref.shape[0]
    cch = _tile(length, NAT_CTX_CHUNK)
    n_cch = length // cch
    c_unroll = NAT_UNROLL if n_cch % NAT_UNROLL == 0 else 1
    span = NAT_BLOCK_ROWS + kh
    n_blk = rows // NAT_BLOCK_ROWS
    b_unroll = NAT_UNROLL if n_blk % NAT_UNROLL == 0 else 1
    qn, kn = NAT_BLOCK_ROWS * GRID_W, span * GRID_W
    lanes = [slice(h * head_dim, (h + 1) * head_dim) for h in range(heads)]

    def block_diag(parts):
        zero = jnp.zeros_like(parts[0])
        return jnp.concatenate([jnp.concatenate([p if j == i else zero for j in range(heads)], axis=-1)
                                for i, p in enumerate(parts)], axis=0)

    kc = block_diag([kc_ref[:, ls].astype(BF16) for ls in lanes])
    vc = block_diag([vc_ref[:, ls].astype(BF16) for ls in lanes])

    def ctx_group(g, carry):
        rws = [pl.ds(pl.multiple_of((g * c_unroll + u) * cch, cch), cch) for u in range(c_unroll)]
        scores = [_dot_t(q_ref[rw, :], kc) * scale for rw in rws]
        probs = []
        for rw, s in zip(rws, scores):
            per_head = []
            for h in range(heads):
                sh = s[:, h * past:(h + 1) * past]
                m = jnp.max(sh, axis=-1, keepdims=True)
                p = jnp.exp(sh - m)
                mc_ref[h, rw, :] = m
                lc_ref[h, rw, :] = jnp.sum(p, axis=-1, keepdims=True)
                per_head.append(p.astype(BF16))
            probs.append(jnp.concatenate(per_head, axis=-1))
        for rw, p in zip(rws, probs):
            oc_ref[rw, :] = _dot(p, vc)
        return carry

    lax.fori_loop(0, n_cch // c_unroll, ctx_group, 0)

    def block_group(g, carry):
        qrows, wins, kinds = [], [], []
        for u in range(b_unroll):
            blk = g * b_unroll + u
            r0 = blk * NAT_BLOCK_ROWS
            start = jnp.clip(r0 - kh // 2, 0, rows - span)
            qrows.append(pl.ds(pl.multiple_of(r0 * GRID_W, qn), qn))
            wins.append(pl.ds(pl.multiple_of(start * GRID_W, GRID_W), kn))
            kinds.append(jnp.where(blk == 0, 0, jnp.where(blk == n_blk - 1, 2, 1)))
        scores = [_dot_t(q_ref[qr, :], block_diag([k_ref[w, ls] for ls in lanes])) * scale
                  for qr, w in zip(qrows, wins)]
        probs, wcs, denoms = [], [], []
        for qr, s, kd in zip(qrows, scores, kinds):
            per_head, wc_h, denom_h = [], [], []
            for h in range(heads):
                sh = s[:, h * kn:(h + 1) * kn] + bias_ref[h, kd]
                mc = mc_ref[h, qr, :]
                m = jnp.maximum(jnp.max(sh, axis=-1, keepdims=True), mc)
                p = jnp.exp(sh - m)
                wc = jnp.exp(mc - m)
                per_head.append(p.astype(BF16))
                wc_h.append(wc)
                denom_h.append(jnp.sum(p, axis=-1, keepdims=True) + lc_ref[h, qr, :] * wc)
            probs.append(jnp.concatenate(per_head, axis=-1))
            wcs.append(wc_h)
            denoms.append(denom_h)
        accs = [_dot(p, block_diag([v_ref[w, ls] for ls in lanes])) for p, w in zip(probs, wins)]
        for qr, acc, wc_h, denom_h in zip(qrows, accs, wcs, denoms):
            for h, ls in enumerate(lanes):
                o_ref[qr, ls] = ((acc[:, ls] + oc_ref[qr, ls] * wc_h[h]) / denom_h[h]).astype(BF16)
        return carry

    lax.fori_loop(0, n_blk // b_unroll, block_group, 0)


def _nat_bias_strips(rpb, kh, kw):
    n_heads = rpb.shape[0]
    qc = np.arange(GRID_W)[:, None]
    kc = np.arange(GRID_W)[None, :]
    cstart = np.clip(qc - kw // 2, 0, GRID_W - kw)
    ok = (kc >= cstart) & (kc < cstart + kw)
    padded = jnp.pad(rpb, ((0, 0), (0, 0), (GRID_W - kw, GRID_W - kw)))
    toep = jnp.stack([padded[:, :, GRID_W - 1 - q:2 * GRID_W - 1 - q] for q in range(GRID_W)], axis=2)
    toep = jnp.where(ok[None, None], toep, -jnp.inf)
    return jnp.stack([jnp.transpose(toep[:, o:o + kh], (0, 2, 1, 3)).reshape(n_heads, GRID_W, kh * GRID_W)
                      for o in range(kh)], axis=1)


def _nat_block_tiles(strips, rows):
    n_heads, kh = strips.shape[:2]
    span = NAT_BLOCK_ROWS + kh
    n_blk = rows // NAT_BLOCK_ROWS
    assert rows % NAT_BLOCK_ROWS == 0 and rows >= span and NAT_BLOCK_ROWS >= kh // 2

    def ninf(width):
        return jnp.full((n_heads, GRID_W, width * GRID_W), -jnp.inf, F32)

    kinds = []
    for blk in (0, min(1, n_blk - 1), n_blk - 1):
        r0 = blk * NAT_BLOCK_ROWS
        start = min(max(r0 - kh // 2, 0), rows - span)
        slabs = []
        for r in range(r0, r0 + NAT_BLOCK_ROWS):
            rs = min(max(r - kh // 2, 0), rows - kh)
            lead = rs - start
            slabs.append(jnp.concatenate([ninf(lead), strips[:, rs - r + kh - 1], ninf(span - kh - lead)], axis=-1))
        kinds.append(jnp.concatenate(slabs, axis=1))
    return jnp.stack(kinds, axis=1)


def _neighbourhood_attention(proj3, col0, cache_k4, cache_v4, layer, tiles, kh, n_heads, head_dim):
    bsz, length, _ = proj3.shape
    rows = length // GRID_W
    past = cache_k4.shape[2]
    hp = _heads_per_step(n_heads, head_dim, col0, HEADS_PER_STEP_NAT)
    bw = hp * head_dim
    nb = n_heads // hp
    base = col0 // bw
    blocks = (3 * _nbytes((length, bw), proj3.dtype) + 2 * _nbytes((past, bw), F32)
              + _nbytes((hp,) + tiles.shape[1:], F32) + _nbytes((length, bw), BF16))
    scratch = [pltpu.VMEM((hp, length, 1), F32), pltpu.VMEM((hp, length, 1), F32), pltpu.VMEM((length, bw), F32)]
    temps = (2 * hp * _nbytes((length, V7X_LANES), F32) + _nbytes((length, bw), F32)
             + 6 * hp * NAT_UNROLL * _nbytes((_tile(length, NAT_CTX_CHUNK), past), F32)
             + 6 * hp * NAT_UNROLL * _nbytes(tiles.shape[2:], F32))

    def col(g):
        return pl.BlockSpec((None, length, bw), lambda b, j: (b, 0, base + g * nb + j))

    cache_spec = pl.BlockSpec((None, None, past, bw), lambda b, j: (b, layer, 0, j))
    return pl.pallas_call(
        functools.partial(_nat_kernel, rows=rows, kh=kh, heads=hp, head_dim=head_dim),
        out_shape=jax.ShapeDtypeStruct((bsz, length, n_heads * head_dim), BF16),
        grid=(bsz, nb),
        in_specs=[col(0), col(1), col(2), cache_spec, cache_spec,
                  pl.BlockSpec((hp,) + tiles.shape[1:], lambda b, j: (j, 0, 0, 0))],
        out_specs=pl.BlockSpec((None, length, bw), lambda b, j: (b, 0, j)),
        scratch_shapes=scratch,
        compiler_params=_params(("arbitrary", "arbitrary"), blocks, temps),
        name="neighbourhood_attention",
    )(proj3, proj3, proj3, cache_k4, cache_v4, tiles)


def _dft_matrices(length):
    n = 2 * length
    split = min(DFT_SPLIT, length)
    s = np.arange(length, dtype=np.int64)
    pa = ((np.arange(length // split, dtype=np.int64)[:, None] * split * s[None, :]) % n).astype(np.int32)
    pb = ((np.arange(split, dtype=np.int64)[:, None] * s[None, :]) % n).astype(np.int32)
    ta = jnp.asarray(pa).astype(F32) * (2.0 * math.pi / n)
    tb = jnp.asarray(pb).astype(F32) * (2.0 * math.pi / n)
    ca, sa, cb, sb = jnp.cos(ta), jnp.sin(ta), jnp.cos(tb), jnp.sin(tb)
    cosm = (ca[:, None, :] * cb[None, :, :] - sa[:, None, :] * sb[None, :, :]).reshape(length, length)
    msin = -(sa[:, None, :] * cb[None, :, :] + ca[:, None, :] * sb[None, :, :]).reshape(length, length)
    k_idx = lax.broadcasted_iota(jnp.int32, (length, length), 0)
    s_idx = lax.broadcasted_iota(jnp.int32, (length, length), 1)
    sinm = jnp.where(k_idx == 0, jnp.where(s_idx % 2 == 0, 1.0, -1.0), msin)
    sinm_t = jnp.where(s_idx == 0, jnp.where(k_idx % 2 == 0, 1.0, -1.0), msin)
    return cosm.astype(BF16), sinm.astype(BF16), sinm_t.astype(BF16)


def _filter_kernel(bands_ref, w1_ref, b1_ref, f_ref, w2_ref, b2_ref, w3_ref, dl_ref, o_ref,
                   *, length, tl, emb_bands, width):
    hi = lax.Precision.HIGHEST
    pos = (lax.broadcasted_iota(jnp.int32, (tl, V7X_LANES), 0) + pl.program_id(0) * tl).astype(F32)
    lane = lax.broadcasted_iota(jnp.int32, (tl, V7X_LANES), 1)
    t = pos * (1.0 / (length - 1))
    ang = ((2.0 * math.pi / length) * pos) * bands_ref[...]
    feat = jnp.where(lane == 0, t,
                     jnp.where(lane <= emb_bands, jnp.cos(ang),
                               jnp.where(lane <= 2 * emb_bands, -jnp.sin(ang), 0.0)))
    h = jnp.sin(f_ref[0:1, :] * (jnp.dot(feat, w1_ref[...], precision=hi, preferred_element_type=F32)
                                 + b1_ref[...]))
    h = jnp.sin(f_ref[1:2, :] * (jnp.dot(h, w2_ref[...], precision=hi, preferred_element_type=F32)
                                 + b2_ref[...]))
    h = jnp.dot(h, w3_ref[...], precision=hi, preferred_element_type=F32)
    decay = jnp.exp(-(t[:, 0:1]) * dl_ref[...])
    col = lax.broadcasted_iota(jnp.int32, h.shape, 1)
    first = (pos[:, 0:1] == 0.0) & (col >= width)
    o_ref[...] = jnp.where(first, 0.0, h * decay).astype(BF16)


def _hyena_filters(length, w1, b1, freq, w2, b2, w3):
    emb, fo = w1.shape
    width = w3.shape[1] // 2
    emb_bands = (emb - 1) // 2
    tl = _tile(length, TM_DFT)
    bands = jnp.linspace(1e-4, emb_bands - 1, emb_bands, dtype=F32)
    bands_row = jnp.zeros((1, V7X_LANES), F32).at[0, 1:1 + emb_bands].set(bands)
    bands_row = bands_row.at[0, 1 + emb_bands:1 + 2 * emb_bands].set(bands)
    w1p = jnp.zeros((V7X_LANES, fo), F32).at[:emb].set(w1)
    deltas = jnp.abs(jnp.linspace(math.log(HY_TARGET) / HY_SLOW, math.log(HY_TARGET) / HY_FAST, width, dtype=F32))
    dl = jnp.concatenate([deltas, deltas]).reshape(1, 2 * width)
    full = lambda shape: pl.BlockSpec(shape, lambda i: (0,) * len(shape))
    blocks = _nbytes((tl, 2 * width), BF16) + _nbytes((fo + 2, 2 * width), F32) + _nbytes((V7X_LANES + fo, fo), F32)
    return pl.pallas_call(
        functools.partial(_filter_kernel, length=length, tl=tl, emb_bands=emb_bands, width=width),
        out_shape=jax.ShapeDtypeStruct((length, 2 * width), BF16),
        grid=(length // tl,),
        in_specs=[full((1, V7X_LANES)), full((V7X_LANES, fo)), full((1, fo)), full((2, fo)),
                  full((fo, fo)), full((1, fo)), full((fo, 2 * width)), full((1, 2 * width))],
        out_specs=pl.BlockSpec((tl, 2 * width), lambda i: (i, 0)),
        compiler_params=_params(("arbitrary",), blocks, 6 * _nbytes((tl, 2 * width), F32)),
        name="hyena_filters",
    )(bands_row, w1p, b1.reshape(1, fo), freq, w2, b2.reshape(1, fo), w3, dl)


def _spectrum_kernel(c_ref, s_ref, hf_ref, hb_ref, fr_ref, fi_ref, *, tm, norm):
    cm, sm, hf, hb = c_ref[...], s_ref[...], hf_ref[...], hb_ref[...]
    row = lax.broadcasted_iota(jnp.int32, fr_ref.shape, 0) + pl.program_id(0) * tm
    packed = row == 0
    wgt = jnp.where(packed, 0.5 * norm, norm)
    bi = _dot(sm, hb)
    fr_ref[...] = (_dot(cm, hf) + _dot(cm, hb)) * wgt
    fi_ref[...] = (_dot(sm, hf) + jnp.where(packed, bi, -bi)) * wgt


def _filter_spectrum(cosm, sinm, hcat):
    length = cosm.shape[0]
    width = hcat.shape[1] // 2
    tm, tn = _tile(length, TM_DFT), _tile(width, TN_DFT)
    nb = width // tn
    blocks = 2 * _nbytes((tm, length), BF16) + 2 * _nbytes((length, tn), BF16) + 2 * _nbytes((tm, tn), F32)
    out = jax.ShapeDtypeStruct((length, width), F32)
    ospec = pl.BlockSpec((tm, tn), lambda i, j: (i, j))
    return pl.pallas_call(
        functools.partial(_spectrum_kernel, tm=tm, norm=1.0 / length),
        out_shape=(out, out),
        grid=(length // tm, nb),
        in_specs=[pl.BlockSpec((tm, length), lambda i, j: (i, 0)),
                  pl.BlockSpec((tm, length), lambda i, j: (i, 0)),
                  pl.BlockSpec((length, tn), lambda i, j: (0, j)),
                  pl.BlockSpec((length, tn), lambda i, j: (0, nb + j))],
        out_specs=(ospec, ospec),
        compiler_params=_params(("arbitrary", "arbitrary"), blocks, 16 * _nbytes((tm, tn), F32)),
        name="filter_spectrum",
    )(cosm, sinm, hcat, hcat)


def _dft_fwd_kernel(c_ref, s_ref, z_ref, fr_ref, fi_ref, yr_ref, yi_ref, *, tm):
    z = z_ref[...]
    zr, zi = _dot(c_ref[...], z), _dot(s_ref[...], z)
    fr, fi = fr_ref[...], fi_ref[...]
    packed = (lax.broadcasted_iota(jnp.int32, zr.shape, 0) + pl.program_id(0) * tm) == 0
    yr_ref[...] = jnp.where(packed, zr * fr, zr * fr - zi * fi).astype(BF16)
    yi_ref[...] = jnp.where(packed, zi * fi, zr * fi + zi * fr).astype(BF16)


def _dft_forward(cosm, sinm, z, fr, fi):
    bsz, length, width = z.shape
    tm, tn = _tile(length, TM_DFT), _tile(width, TN_DFT)
    blocks = (2 * _nbytes((tm, length), BF16) + _nbytes((length, tn), BF16) + 2 * _nbytes((tm, tn), F32)
              + 2 * _nbytes((tm, tn), BF16))
    out = jax.ShapeDtypeStruct((bsz, length, width), BF16)
    ospec = pl.BlockSpec((None, tm, tn), lambda i, b, j: (b, i, j))
    return pl.pallas_call(
        functools.partial(_dft_fwd_kernel, tm=tm),
        out_shape=(out, out),
        grid=(length // tm, bsz, width // tn),
        in_specs=[pl.BlockSpec((tm, length), lambda i, b, j: (i, 0)),
                  pl.BlockSpec((tm, length), lambda i, b, j: (i, 0)),
                  pl.BlockSpec((None, length, tn), lambda i, b, j: (b, 0, j)),
                  pl.BlockSpec((tm, tn), lambda i, b, j: (i, j)),
                  pl.BlockSpec((tm, tn), lambda i, b, j: (i, j))],
        out_specs=(ospec, ospec),
        compiler_params=_params(("arbitrary",) * 3, blocks, 16 * _nbytes((tm, tn), F32)),
        name="dft_forward",
    )(cosm, sinm, z, fr, fi)


def _dft_inv_kernel(c_ref, st_ref, yr_ref, yi_ref, x0_ref, z_ref, b_ref, o_ref):
    y = _dot(c_ref[...], yr_ref[...]) + _dot(st_ref[...], yi_ref[...])
    y = y + z_ref[...].astype(F32) * b_ref[...]
    o_ref[...] = (x0_ref[...].astype(F32) * y).astype(BF16)


def _dft_inverse(cosm, sinm_t, yr, yi, x0, z, bias):
    bsz, length, width = z.shape
    tm, tn = _tile(length, TM_DFT), _tile(width, TN_DFT)
    blocks = (2 * _nbytes((tm, length), BF16) + 2 * _nbytes((length, tn), BF16)
              + 3 * _nbytes((tm, tn), BF16) + _nbytes((1, tn), F32))
    tile = pl.BlockSpec((None, tm, tn), lambda i, b, j: (b, i, j))
    panel = pl.BlockSpec((None, length, tn), lambda i, b, j: (b, 0, j))
    return pl.pallas_call(
        _dft_inv_kernel,
        out_shape=jax.ShapeDtypeStruct((bsz, length, width), BF16),
        grid=(length // tm, bsz, width // tn),
        in_specs=[pl.BlockSpec((tm, length), lambda i, b, j: (i, 0)),
                  pl.BlockSpec((tm, length), lambda i, b, j: (i, 0)),
                  panel, panel, tile, tile,
                  pl.BlockSpec((1, tn), lambda i, b, j: (0, j))],
        out_specs=tile,
        compiler_params=_params(("arbitrary",) * 3, blocks, 16 * _nbytes((tm, tn), F32)),
        name="dft_inverse",
    )(cosm, sinm_t, yr, yi, x0, z, bias.reshape(1, width))


def _merge_kernel(ya_ref, yb_ref, yc_ref, ga_ref, gb_ref, gc_ref, wa_ref, wb_ref, wc_ref, o_ref):
    ga = jax.nn.sigmoid(ga_ref[...].astype(F32))
    gb = jax.nn.sigmoid(gb_ref[...].astype(F32))
    gc = jax.nn.sigmoid(gc_ref[...].astype(F32))
    m = ga * _dot(ya_ref[...], wa_ref[...])
    m = m + gb * _dot(yb_ref[...], wb_ref[...])
    m = m + gc * _dot(yc_ref[...], wc_ref[...])
    o_ref[...] = m.astype(BF16)


def _merge(ya, yb, yc, proj, gate_col0, wa, wb, wc):
    t = ya.shape[0]
    d = wa.shape[1]
    tm, tn = _tile(t, TM_MM), _tile(d, TN_HALF)
    nb = d // tn
    base = gate_col0 // tn
    ka, kb, kc = ya.shape[1], yb.shape[1], yc.shape[1]
    blocks = (_nbytes((tm, ka + kb + kc), BF16) + 3 * _nbytes((tm, tn), proj.dtype)
              + _nbytes((ka + kb + kc, tn), BF16) + _nbytes((tm, tn), BF16))

    def gate(g):
        return pl.BlockSpec((tm, tn), lambda i, j: (i, base + g * nb + j))

    def panel(k):
        return pl.BlockSpec((tm, k), lambda i, j: (i, 0))

    def wcol(k):
        return pl.BlockSpec((k, tn), lambda i, j: (0, j))

    return pl.pallas_call(
        _merge_kernel,
        out_shape=jax.ShapeDtypeStruct((t, d), BF16),
        grid=(t // tm, nb),
        in_specs=[panel(ka), panel(kb), panel(kc), gate(0), gate(1), gate(2), wcol(ka), wcol(kb), wcol(kc)],
        out_specs=pl.BlockSpec((tm, tn), lambda i, j: (i, j)),
        compiler_params=_params(("arbitrary", "arbitrary"), blocks, 16 * _nbytes((tm, tn), F32)),
        name="gated_merge",
    )(ya, yb, yc, proj, proj, proj, wa, wb, wc)


def _lane_partial_sum(v):
    return functools.reduce(jnp.add, [v[:, t * V7X_LANES:(t + 1) * V7X_LANES]
                                      for t in range(v.shape[1] // V7X_LANES)])


def _mm_epilogue_kernel(a_ref, b_ref, x_ref, gpost_ref, gate_ref, *rest, n_tiles, n_chunks, by_columns,
                        with_next):
    if with_next:
        gnext_ref, sc_ref, sh_ref, xo_ref, h_ref, acc_ref = rest
    else:
        xo_ref, acc_ref = rest
    i = pl.program_id(0)
    k = pl.program_id(1)
    _, n_panels, tm, pw = acc_ref.shape
    d = n_panels * pw
    rc = tm // n_chunks
    ew = _tile(pw, EPI_COL_PIECE)
    chunk = jnp.minimum(k, n_chunks - 1)

    @pl.when((i == 0) & (k == 0))
    def _():
        acc_ref[...] = jnp.zeros_like(acc_ref)

    def accumulate_stages(slot):
        def whole():
            acc_ref[slot, k] = _dot(a_ref[...], b_ref[...])

        def panel(c):
            part = _dot(a_ref[...], b_ref[:, c * pw:(c + 1) * pw])
            acc_ref[slot, c] = jnp.where(k == 0, part, acc_ref[slot, c] + part)

        return [whole] if by_columns else [functools.partial(panel, c) for c in range(n_panels)]

    def epilogue_stages(slot):
        rows = pl.ds(pl.multiple_of(chunk * rc, rc), rc)
        pieces = [(c, slice(e * ew, (e + 1) * ew), slice(c * pw + e * ew, c * pw + (e + 1) * ew))
                  for c in range(n_panels) for e in range(pw // ew)]
        rstd = {}

        def scale_of(ssq):
            return lax.rsqrt(jnp.sum(ssq, axis=-1, keepdims=True) / d + RMS_EPS)

        def product_norm():
            rstd["post"] = scale_of(functools.reduce(
                jnp.add, [_lane_partial_sum(jnp.square(acc_ref[slot, c, rows, ps])) for c, ps, _ in pieces]))

        def residual():
            ssq = jnp.zeros((rc, V7X_LANES), F32)
            for c, ps, cs in pieces:
                y = acc_ref[slot, c, rows, ps] * rstd["post"] * gpost_ref[:, cs]
                xn = x_ref[:, cs] + gate_ref[:, cs] * y
                xo_ref[:, cs] = xn
                ssq = ssq + _lane_partial_sum(jnp.square(xn))
            rstd["next"] = scale_of(ssq)

        def next_prenorm():
            for _, _, cs in pieces:
                hn = xo_ref[:, cs] * rstd["next"] * gnext_ref[:, cs]
                h_ref[:, cs] = (hn * (1.0 + sc_ref[:, cs]) + sh_ref[:, cs]).astype(BF16)

        return [product_norm, residual] + ([next_prenorm] if with_next else [])

    def run(*stage_lists):
        for stage in itertools.chain(*stage_lists):
            stage()

    @pl.when(i == 0)
    def _():
        run(accumulate_stages(0))

    inner = (i > 0) & (i < n_tiles)

    @pl.when(inner & (i % 2 == 0))
    def _():
        run(epilogue_stages(1), accumulate_stages(0))

    @pl.when(inner & (i % 2 == 1))
    def _():
        run(epilogue_stages(0), accumulate_stages(1))

    @pl.when(i == n_tiles)
    def _():
        run(epilogue_stages((n_tiles - 1) % 2))


def _matmul_epilogue(a, b, x2, gain_post, mod_l, gate_idx, rows_per_batch, nxt):
    t, kdim = a.shape
    d = b.shape[1]
    tm, tpb = _row_tiling(mod_l, t, rows_per_batch, TM_EPI)
    n_tiles = t // tm
    by_columns = kdim <= EPI_FULL_DEPTH_MAX
    if by_columns:
        pw = _tile(d, EPI_COL_PIECE)
        steps = d // pw
        a_spec = pl.BlockSpec((tm, kdim), lambda i, k: (jnp.minimum(i, n_tiles - 1), 0))
        b_spec = pl.BlockSpec((kdim, pw), lambda i, k: (0, jnp.where(i == n_tiles, steps - 1, k)))
        blocks = _nbytes((tm, kdim), BF16) + _nbytes((kdim, pw), BF16)
    else:
        fits = [w for w in EPI_K_TILES if kdim % w == 0]
        tk = next((w for w in fits if kdim // w >= EPI_ROW_CHUNKS), fits[-1])
        pw = _tile(d, EPI_COL_CHUNK)
        steps = kdim // tk
        a_spec = pl.BlockSpec((tm, tk), lambda i, k: (jnp.minimum(i, n_tiles - 1),
                                                      jnp.where(i == n_tiles, steps - 1, k)))
        b_spec = pl.BlockSpec((tk, d), lambda i, k: (jnp.where(i == n_tiles, steps - 1, k), 0))
        blocks = _nbytes((tm, tk), BF16) + _nbytes((tk, d), BF16)
    n_chunks = max(n for n in range(1, min(steps, EPI_ROW_CHUNKS) + 1) if tm % (n * 2 * V7X_SUBLANES) == 0)
    rc = tm // n_chunks

    def prev_tile(i):
        return jnp.maximum(i - 1, 0)

    def chunk_index(i, k):
        return jnp.where(i == 0, 0, (i - 1) * n_chunks + jnp.minimum(k, n_chunks - 1))

    def vec(which):
        return pl.BlockSpec((None, None, 1, d), lambda i, k: (prev_tile(i) // tpb, which, 0, 0))

    row = pl.BlockSpec((1, d), lambda i, k: (0, 0))
    chunk_rows = pl.BlockSpec((rc, d), lambda i, k: (chunk_index(i, k), 0))
    in_specs = [a_spec, b_spec, chunk_rows, row, vec(gate_idx)]
    args = [a, b, x2, gain_post.reshape(1, d), mod_l]
    blocks += 2 * _nbytes((rc, d), F32) + 5 * _nbytes((1, d), F32)
    if nxt is None:
        out_shape = jax.ShapeDtypeStruct((t, d), F32)
        out_specs = chunk_rows
    else:
        gain_next, mod_next, sc_idx, sh_idx = nxt
        in_specs += [row, vec(sc_idx), vec(sh_idx)]
        args += [gain_next.reshape(1, d), mod_next, mod_next]
        out_shape = (jax.ShapeDtypeStruct((t, d), F32), jax.ShapeDtypeStruct((t, d), BF16))
        out_specs = (chunk_rows, chunk_rows)
        blocks += _nbytes((rc, d), BF16)
    temps = (_nbytes((2, tm, d), F32) + 3 * _nbytes((tm, pw), F32)
             + 8 * _nbytes((rc, _tile(pw, EPI_COL_PIECE)), F32))
    return pl.pallas_call(
        functools.partial(_mm_epilogue_kernel, n_tiles=n_tiles, n_chunks=n_chunks, by_columns=by_columns,
                          with_next=nxt is not None),
        out_shape=out_shape,
        grid=(n_tiles + 1, steps),
        in_specs=in_specs,
        out_specs=out_specs,
        scratch_shapes=[pltpu.VMEM((2, d // pw, tm, pw), F32)],
        compiler_params=_params(("arbitrary", "arbitrary"), blocks, temps),
        name="matmul_norm_residual",
    )(*args)


def _ffn_up_kernel(h_ref, wg_ref, wu_ref, o_ref):
    h = h_ref[...]
    o_ref[...] = (jax.nn.silu(_dot(h, wg_ref[...])) * _dot(h, wu_ref[...])).astype(BF16)


def _ffn_up(h, wg, wu):
    t, d = h.shape
    n = wg.shape[1]
    tm, tn = _tile(t, TM_MM), _tile(n, TN_HALF)
    blocks = _nbytes((tm, d), BF16) + 2 * _nbytes((d, tn), BF16) + _nbytes((tm, tn), BF16)
    return pl.pallas_call(
        _ffn_up_kernel,
        out_shape=jax.ShapeDtypeStruct((t, n), BF16),
        grid=(t // tm, n // tn),
        in_specs=[pl.BlockSpec((tm, d), lambda i, j: (i, 0)),
                  pl.BlockSpec((d, tn), lambda i, j: (0, j)),
                  pl.BlockSpec((d, tn), lambda i, j: (0, j))],
        out_specs=pl.BlockSpec((tm, tn), lambda i, j: (i, j)),
        compiler_params=_params(("arbitrary", "arbitrary"), blocks, 16 * _nbytes((tm, tn), F32)),
        name="ffn_up",
    )(h, wg, wu)


def _pad_cols(w, mult):
    pad = (-w.shape[-1]) % mult
    return jnp.pad(w, ((0, 0), (0, 0), (0, pad))) if pad else w


def _pad_rows(w, mult):
    pad = (-w.shape[-2]) % mult
    return jnp.pad(w, ((0, 0), (0, pad), (0, 0))) if pad else w


def kernel(x_prompt, x_sample, cache_k, cache_v, c, c_ctx, w_mod, b_mod, norm_gains, w_in, conv_a, w_up_a, na_rpb, w_up_b, conv_c, filt_w1, filt_b1, filt_freq, filt_w2, filt_b2, filt_w3, hyena_bias, w_up_c, w_out, w_ffn_gate, w_ffn_up, w_ffn_down):
    depth, d, _ = w_mod.shape
    n_heads, head_dim = cache_k.shape[3], cache_k.shape[4]
    sc_w = conv_a.shape[-1]
    na_w = n_heads * head_dim
    hy_w = hyena_bias.shape[-1]
    kh_full = (na_rpb.shape[2] + 1) // 2
    kw = (na_rpb.shape[3] + 1) // 2
    col_na = 3 * sc_w
    col_hy = col_na + 3 * na_w
    col_gate = col_hy + 3 * hy_w
    assert x_sample.shape[1] % GRID_W == 0 and x_sample.shape[1] // GRID_W >= kh_full
    assert 1 + c.shape[0] <= MOD_ROWS_PAD

    w_in_b = w_in.astype(BF16)
    w_up_a_b, w_up_b_b, w_up_c_b = w_up_a.astype(BF16), w_up_b.astype(BF16), w_up_c.astype(BF16)
    w_out_b = w_out.astype(BF16)
    w_gate_b = _pad_cols(w_ffn_gate.astype(BF16), FFN_PAD)
    w_upf_b = _pad_cols(w_ffn_up.astype(BF16), FFN_PAD)
    w_down_b = _pad_rows(w_ffn_down.astype(BF16), FFN_PAD)

    c_rows = jnp.zeros((MOD_ROWS_PAD, d), F32).at[0].set(c_ctx).at[1:1 + c.shape[0]].set(c)
    mod = _modulation(c_rows, w_mod, b_mod).reshape(depth, MOD_ROWS_PAD, N_MOD, 1, d)
    cache_k4 = cache_k.reshape(cache_k.shape[:3] + (na_w,))
    cache_v4 = cache_v.reshape(cache_v.shape[:3] + (na_w,))

    def run_group(x3, mod_g, latent):
        bsz, length, _ = x3.shape
        proj_dtype = BF16 if latent else F32
        cosm, sinm, sinm_t = _dft_matrices(length)
        x2 = x3.reshape(bsz * length, d)
        h = _prenorm(x2, norm_gains[0, 0], mod_g[0], length)
        ks, vs = [], []
        for l in range(depth):
            proj = _matmul(h, w_in_b[l], proj_dtype)
            proj3 = proj.reshape(bsz, length, proj.shape[1])
            y_sc = _short_conv(proj3, conv_a[l], 0)
            if latent:
                tiles = _nat_block_tiles(_nat_bias_strips(na_rpb[l], kh_full, kw), length // GRID_W)
                y_na = _neighbourhood_attention(proj3, col_na, cache_k4, cache_v4, l, tiles, kh_full,
                                                n_heads, head_dim)
            else:
                y_na = _context_attention(proj3, col_na, n_heads, head_dim)
                ks.append(proj3[:, :, col_na + na_w:col_na + 2 * na_w].reshape(bsz, length, n_heads, head_dim))
                vs.append(proj3[:, :, col_na + 2 * na_w:col_hy].reshape(bsz, length, n_heads, head_dim))
            x0, z = _hyena_pre(proj3, conv_c[l], col_hy)
            hcat = _hyena_filters(length, filt_w1[l], filt_b1[l], filt_freq[l], filt_w2[l], filt_b2[l], filt_w3[l])
            fr, fi = _filter_spectrum(cosm, sinm, hcat)
            yr, yi = _dft_forward(cosm, sinm, z, fr, fi)
            y_hy = _dft_inverse(cosm, sinm_t, yr, yi, x0, z, hyena_bias[l])
            merged = _merge(y_sc.reshape(-1, sc_w), y_na.reshape(-1, na_w), y_hy.reshape(-1, hy_w),
                            proj, col_gate, w_up_a_b[l], w_up_b_b[l], w_up_c_b[l])
            x2, h2 = _matmul_epilogue(merged, w_out_b[l], x2, norm_gains[l, 1], mod_g[l], 2, length,
                                      (norm_gains[l, 2], mod_g[l], 4, 3))
            hidden = _ffn_up(h2, w_gate_b[l], w_upf_b[l])
            if l + 1 < depth:
                x2, h = _matmul_epilogue(hidden, w_down_b[l], x2, norm_gains[l, 3], mod_g[l], 5, length,
                                         (norm_gains[l + 1, 0], mod_g[l + 1], 1, 0))
            else:
                x2 = _matmul_epilogue(hidden, w_down_b[l], x2, norm_gains[l, 3], mod_g[l], 5, length, None)
        return x2.reshape(bsz, length, d), ks, vs

    y_prompt, ks, vs = run_group(x_prompt, mod[:, 0:1], latent=False)
    y_sample, _, _ = run_group(x_sample, mod[:, 1:1 + c.shape[0]], latent=True)
    return (y_prompt, y_sample, jnp.stack(ks, axis=1), jnp.stack(vs, axis=1))
```

```python
import functools
import itertools
import math

import numpy as np
import jax
import jax.numpy as jnp
from jax import lax
from jax.experimental import pallas as pl
from jax.experimental.pallas import tpu as pltpu

F32 = jnp.float32
BF16 = jnp.bfloat16

GRID_W = 64
HY_FAST = 0.3
HY_SLOW = 1.5
HY_TARGET = 1e-2
RMS_EPS = 1e-6
N_MOD = 6
N_BRANCH = 3

V7X_VMEM_BYTES = 64 * 1024 * 1024
V7X_VMEM_RESERVED_BYTES = 6 * 1024 * 1024
V7X_LANES = 128
V7X_SUBLANES = 8
MOD_ROWS_PAD = 16

TM_MM = 1024
TN_MM = 1024
TN_HALF = 512
TM_EPI = 512
TM_EPI_FULL_DEPTH = 1024
EPI_K_TILES = (1024, 512, 256, 128)
EPI_FULL_DEPTH_MAX = 4096
EPI_ROW_CHUNKS = 8
EPI_COL_CHUNK = 1024
EPI_COL_PIECE = 512
TM_DFT = 512
TN_DFT = 512
TN_DFT_FWD = 1024
CONV_TILE_ELEMS = 512 * 1024
FFN_PAD = 1024
HEADS_PER_STEP_CTX = 8
HEADS_PER_STEP_NAT = 2
NAT_CTX_CHUNK = 512
NAT_BLOCK_ROWS = 4
NAT_UNROLL = 4
DFT_SPLIT = 64


def _tile(dim, pref):
    return pref if dim % pref == 0 else dim


def _nbytes(shape, dtype):
    return int(np.prod(shape)) * jnp.dtype(dtype).itemsize


def _params(semantics, block_bytes, temp_bytes=0):
    need = 2 * block_bytes + temp_bytes
    limit = min(V7X_VMEM_BYTES - V7X_VMEM_RESERVED_BYTES, max(need, 16 * 1024 * 1024))
    return pltpu.CompilerParams(dimension_semantics=semantics, vmem_limit_bytes=limit)


def _dot(a, b):
    return jnp.dot(a, b, preferred_element_type=F32)


def _dot_t(a, b):
    return lax.dot_general(a, b, (((1,), (1,)), ((), ())), preferred_element_type=F32)


def _rms(x, gain):
    return x * lax.rsqrt(jnp.mean(x * x, axis=-1, keepdims=True) + RMS_EPS) * gain


def _mod_kernel(c_ref, w_ref, b_ref, o_ref):
    a = jax.nn.silu(c_ref[...]).astype(BF16)
    o_ref[...] = _dot(a, w_ref[...].astype(BF16)) + b_ref[...]


def _modulation(c_rows, w_mod, b_mod):
    depth, d, n = w_mod.shape
    tn = _tile(n, TN_HALF)
    blocks = _nbytes((MOD_ROWS_PAD, d), F32) + _nbytes((d, tn), F32) + _nbytes((MOD_ROWS_PAD + 1, tn), F32)
    return pl.pallas_call(
        _mod_kernel,
        out_shape=jax.ShapeDtypeStruct((depth, MOD_ROWS_PAD, n), F32),
        grid=(depth, n // tn),
        in_specs=[pl.BlockSpec((MOD_ROWS_PAD, d), lambda l, j: (0, 0)),
                  pl.BlockSpec((None, d, tn), lambda l, j: (l, 0, j)),
                  pl.BlockSpec((None, 1, tn), lambda l, j: (l, 0, j))],
        out_specs=pl.BlockSpec((None, MOD_ROWS_PAD, tn), lambda l, j: (l, 0, j)),
        compiler_params=_params(("arbitrary", "arbitrary"), blocks, _nbytes((d, tn), BF16)),
        name="modulation",
    )(c_rows, w_mod, b_mod.reshape(depth, 1, n))


def _row_tiling(mod_l, total_rows, rows_per_batch, pref):
    shared = mod_l.shape[0] == 1
    tm = _tile(total_rows if shared else rows_per_batch, pref)
    return tm, (total_rows if shared else rows_per_batch) // tm


def _vec_spec(d, which, tiles_per_batch):
    return pl.BlockSpec((None, None, 1, d), lambda i: (i // tiles_per_batch, which, 0, 0))


def _prenorm_kernel(x_ref, g_ref, sc_ref, sh_ref, h_ref):
    y = _rms(x_ref[...], g_ref[...])
    h_ref[...] = (y * (1.0 + sc_ref[...]) + sh_ref[...]).astype(BF16)


def _prenorm(x2, gain, mod_l, rows_per_batch):
    t, d = x2.shape
    tm, tpb = _row_tiling(mod_l, t, rows_per_batch, TM_EPI)
    blocks = _nbytes((tm, d), F32) + _nbytes((tm, d), BF16) + 3 * _nbytes((1, d), F32)
    return pl.pallas_call(
        _prenorm_kernel,
        out_shape=jax.ShapeDtypeStruct((t, d), BF16),
        grid=(t // tm,),
        in_specs=[pl.BlockSpec((tm, d), lambda i: (i, 0)),
                  pl.BlockSpec((1, d), lambda i: (0, 0)),
                  _vec_spec(d, 1, tpb),
                  _vec_spec(d, 0, tpb)],
        out_specs=pl.BlockSpec((tm, d), lambda i: (i, 0)),
        compiler_params=_params(("arbitrary",), blocks, 2 * _nbytes((tm, d), F32)),
        name="prenorm",
    )(x2, gain.reshape(1, d), mod_l, mod_l)


def _mm_kernel(a_ref, b_ref, o_ref):
    o_ref[...] = _dot(a_ref[...], b_ref[...]).astype(o_ref.dtype)


def _matmul(a, b, out_dtype):
    m, k = a.shape
    n = b.shape[1]
    tm, tn = _tile(m, TM_MM), _tile(n, TN_MM)
    blocks = _nbytes((tm, k), BF16) + _nbytes((k, tn), BF16) + _nbytes((tm, tn), out_dtype)
    return pl.pallas_call(
        _mm_kernel,
        out_shape=jax.ShapeDtypeStruct((m, n), out_dtype),
        grid=(m // tm, n // tn),
        in_specs=[pl.BlockSpec((tm, k), lambda i, j: (i, 0)),
                  pl.BlockSpec((k, tn), lambda i, j: (0, j))],
        out_specs=pl.BlockSpec((tm, tn), lambda i, j: (i, j)),
        compiler_params=_params(("arbitrary", "arbitrary"), blocks, 3 * _nbytes((tm, tn), F32)),
        name="in_proj",
    )(a, b)


def _dwconv3(s, w):
    n = s.shape[0]
    row = lax.broadcasted_iota(jnp.int32, s.shape, 0)
    prev = jnp.where(row == 0, 0.0, pltpu.roll(s, 1, 0))
    nxt = jnp.where(row == n - 1, 0.0, pltpu.roll(s, n - 1, 0))
    return prev * w[0:1, :] + s * w[1:2, :] + nxt * w[2:3, :]


def _sconv_kernel(b_ref, c_ref, x_ref, w_ref, o_ref):
    s = c_ref[...].astype(F32) * x_ref[...].astype(F32)
    o_ref[...] = (b_ref[...].astype(F32) * _dwconv3(s, w_ref[...])).astype(BF16)


def _short_conv(proj3, conv_w, col0):
    bsz, length, _ = proj3.shape
    width = conv_w.shape[1]
    tc = _tile(width, max(V7X_LANES, CONV_TILE_ELEMS // length))
    nb = width // tc
    base = col0 // tc
    blocks = 3 * _nbytes((length, tc), proj3.dtype) + _nbytes((3, tc), F32) + _nbytes((length, tc), BF16)

    def col(g):
        return pl.BlockSpec((None, length, tc), lambda b, j: (b, 0, base + g * nb + j))

    return pl.pallas_call(
        _sconv_kernel,
        out_shape=jax.ShapeDtypeStruct((bsz, length, width), BF16),
        grid=(bsz, nb),
        in_specs=[col(0), col(1), col(2), pl.BlockSpec((3, tc), lambda b, j: (0, j))],
        out_specs=pl.BlockSpec((None, length, tc), lambda b, j: (b, 0, j)),
        compiler_params=_params(("arbitrary", "arbitrary"), blocks, 6 * _nbytes((length, tc), F32)),
        name="short_conv",
    )(proj3, proj3, proj3, conv_w)


def _hyena_pre_kernel(p0_ref, p1_ref, p2_ref, w0_ref, w1_ref, w2_ref, x0_ref, z_ref):
    x0_ref[...] = _dwconv3(p0_ref[...].astype(F32), w0_ref[...]).astype(BF16)
    x1 = _dwconv3(p1_ref[...].astype(F32), w1_ref[...])
    v = _dwconv3(p2_ref[...].astype(F32), w2_ref[...])
    z_ref[...] = (x1 * v).astype(BF16)


def _hyena_pre(proj3, conv_w, col0):
    bsz, length, _ = proj3.shape
    width = conv_w.shape[1] // 3
    tc = _tile(width, max(V7X_LANES, CONV_TILE_ELEMS // length))
    nb = width // tc
    base = col0 // tc
    blocks = (3 * _nbytes((length, tc), proj3.dtype) + 3 * _nbytes((3, tc), F32)
              + 2 * _nbytes((length, tc), BF16))

    def col(g):
        return pl.BlockSpec((None, length, tc), lambda b, j: (b, 0, base + g * nb + j))

    def wcol(g):
        return pl.BlockSpec((3, tc), lambda b, j: (0, g * nb + j))

    out = jax.ShapeDtypeStruct((bsz, length, width), BF16)
    ospec = pl.BlockSpec((None, length, tc), lambda b, j: (b, 0, j))
    return pl.pallas_call(
        _hyena_pre_kernel,
        out_shape=(out, out),
        grid=(bsz, nb),
        in_specs=[col(0), col(1), col(2), wcol(0), wcol(1), wcol(2)],
        out_specs=(ospec, ospec),
        compiler_params=_params(("arbitrary", "arbitrary"), blocks, 8 * _nbytes((length, tc), F32)),
        name="hyena_pre",
    )(proj3, proj3, proj3, conv_w, conv_w, conv_w)


def _softmax_pv(parts):
    m = functools.reduce(jnp.maximum, [jnp.max(s, axis=-1, keepdims=True) for s, _ in parts])
    ps = [jnp.exp(s - m) for s, _ in parts]
    denom = functools.reduce(jnp.add, [jnp.sum(p, axis=-1, keepdims=True) for p in ps])
    acc = functools.reduce(jnp.add, [_dot(p.astype(BF16), v) for p, (_, v) in zip(ps, parts)])
    return acc / denom


def _ctx_attn_kernel(q_ref, k_ref, v_ref, o_ref, *, heads, head_dim):
    scale = head_dim ** -0.5
    for h in range(heads):
        sl = slice(h * head_dim, (h + 1) * head_dim)
        q = q_ref[:, sl].astype(BF16)
        k = k_ref[:, sl].astype(BF16)
        v = v_ref[:, sl].astype(BF16)
        o_ref[:, sl] = _softmax_pv([(_dot_t(q, k) * scale, v)]).astype(BF16)


def _context_attention(proj3, col0, n_heads, head_dim):
    bsz, length, _ = proj3.shape
    hp = _heads_per_step(n_heads, head_dim, col0, HEADS_PER_STEP_CTX)
    bw = hp * head_dim
    nb = n_heads // hp
    base = col0 // bw
    blocks = 3 * _nbytes((length, bw), proj3.dtype) + _nbytes((length, bw), BF16)

    def col(g):
        return pl.BlockSpec((None, length, bw), lambda b, j: (b, 0, base + g * nb + j))

    return pl.pallas_call(
        functools.partial(_ctx_attn_kernel, heads=hp, head_dim=head_dim),
        out_shape=jax.ShapeDtypeStruct((bsz, length, n_heads * head_dim), BF16),
        grid=(bsz, nb),
        in_specs=[col(0), col(1), col(2)],
        out_specs=pl.BlockSpec((None, length, bw), lambda b, j: (b, 0, j)),
        compiler_params=_params(("arbitrary", "arbitrary"), blocks, 8 * _nbytes((length, length), F32)),
        name="context_attention",
    )(proj3, proj3, proj3)


def _heads_per_step(n_heads, head_dim, col0, cap):
    return max(h for h in range(1, cap + 1) if n_heads % h == 0 and col0 % (h * head_dim) == 0)


def _nat_kernel(q_ref, k_ref, v_ref, kc_ref, vc_ref, bias_ref, o_ref, mc_ref, lc_ref, oc_ref,
                *, rows, kh, heads, head_dim):
    scale = head_dim ** -0.5
    length = rows * GRID_W
    past = kc_ref.shape[0]
    cch = _tile(length, NAT_CTX_CHUNK)
    n_cch = length // cch
    c_unroll = NAT_UNROLL if n_cch % NAT_UNROLL == 0 else 1
    span = NAT_BLOCK_ROWS + kh
    n_blk = rows // NAT_BLOCK_ROWS
    b_unroll = NAT_UNROLL if n_blk % NAT_UNROLL == 0 else 1
    qn, kn = NAT_BLOCK_ROWS * GRID_W, span * GRID_W
    lanes = [slice(h * head_dim, (h + 1) * head_dim) for h in range(heads)]

    def block_diag(parts):
        zero = jnp.zeros_like(parts[0])
        return jnp.concatenate([jnp.concatenate([p if j == i else zero for j in range(heads)], axis=-1)
                                for i, p in enumerate(parts)], axis=0)

    kc = block_diag([kc_ref[:, ls].astype(BF16) for ls in lanes])
    vc = block_diag([vc_ref[:, ls].astype(BF16) for ls in lanes])

    def ctx_group(g, carry):
        rws = [pl.ds(pl.multiple_of((g * c_unroll + u) * cch, cch), cch) for u in range(c_unroll)]
        scores = [_dot_t(q_ref[rw, :], kc) * scale for rw in rws]
        probs = []
        for rw, s in zip(rws, scores):
            per_head = []
            for h in range(heads):
                sh = s[:, h * past:(h + 1) * past]
                m = jnp.max(sh, axis=-1, keepdims=True)
                p = jnp.exp(sh - m)
                mc_ref[h, rw, :] = m
                lc_ref[h, rw, :] = jnp.sum(p, axis=-1, keepdims=True)
                per_head.append(p.astype(BF16))
            probs.append(jnp.concatenate(per_head, axis=-1))
        for rw, p in zip(rws, probs):
            oc_ref[rw, :] = _dot(p, vc)
        return carry

    lax.fori_loop(0, n_cch // c_unroll, ctx_group, 0)

    def block_group(g, carry):
        qrows, wins, kinds = [], [], []
        for u in range(b_unroll):
            blk = g * b_unroll + u
            r0 = blk * NAT_BLOCK_ROWS
            start = jnp.clip(r0 - kh // 2, 0, rows - span)
            qrows.append(pl.ds(pl.multiple_of(r0 * GRID_W, qn), qn))
            wins.append(pl.ds(pl.multiple_of(start * GRID_W, GRID_W), kn))
            kinds.append(jnp.where(blk == 0, 0, jnp.where(blk == n_blk - 1, 2, 1)))
        scores = [_dot_t(q_ref[qr, :], block_diag([k_ref[w, ls] for ls in lanes])) * scale
                  for qr, w in zip(qrows, wins)]
        probs, wcs, denoms = [], [], []
        for qr, s, kd in zip(qrows, scores, kinds):
            per_head, wc_h, denom_h = [], [], []
            for h in range(heads):
                sh = s[:, h * kn:(h + 1) * kn] + bias_ref[h, kd]
                mc = mc_ref[h, qr, :]
                m = jnp.maximum(jnp.max(sh, axis=-1, keepdims=True), mc)
                p = jnp.exp(sh - m)
                wc = jnp.exp(mc - m)
                per_head.append(p.astype(BF16))
                wc_h.append(wc)
                denom_h.append(jnp.sum(p, axis=-1, keepdims=True) + lc_ref[h, qr, :] * wc)
            probs.append(jnp.concatenate(per_head, axis=-1))
            wcs.append(wc_h)
            denoms.append(denom_h)
        accs = [_dot(p, block_diag([v_ref[w, ls] for ls in lanes])) for p, w in zip(probs, wins)]
        for qr, acc, wc_h, denom_h in zip(qrows, accs, wcs, denoms):
            for h, ls in enumerate(lanes):
                o_ref[qr, ls] = ((acc[:, ls] + oc_ref[qr, ls] * wc_h[h]) / denom_h[h]).astype(BF16)
        return carry

    lax.fori_loop(0, n_blk // b_unroll, block_group, 0)


def _nat_bias_strips(rpb, kh, kw):
    n_heads = rpb.shape[0]
    qc = np.arange(GRID_W)[:, None]
    kc = np.arange(GRID_W)[None, :]
    cstart = np.clip(qc - kw // 2, 0, GRID_W - kw)
    ok = (kc >= cstart) & (kc < cstart + kw)
    padded = jnp.pad(rpb, ((0, 0), (0, 0), (GRID_W - kw, GRID_W - kw)))
    toep = jnp.stack([padded[:, :, GRID_W - 1 - q:2 * GRID_W - 1 - q] for q in range(GRID_W)], axis=2)
    toep = jnp.where(ok[None, None], toep, -jnp.inf)
    return jnp.stack([jnp.transpose(toep[:, o:o + kh], (0, 2, 1, 3)).reshape(n_heads, GRID_W, kh * GRID_W)
                      for o in range(kh)], axis=1)


def _nat_block_tiles(strips, rows):
    n_heads, kh = strips.shape[:2]
    span = NAT_BLOCK_ROWS + kh
    n_blk = rows // NAT_BLOCK_ROWS
    assert rows % NAT_BLOCK_ROWS == 0 and rows >= span and NAT_BLOCK_ROWS >= kh // 2

    def ninf(width):
        return jnp.full((n_heads, GRID_W, width * GRID_W), -jnp.inf, F32)

    kinds = []
    for blk in (0, min(1, n_blk - 1), n_blk - 1):
        r0 = blk * NAT_BLOCK_ROWS
        start = min(max(r0 - kh // 2, 0), rows - span)
        slabs = []
        for r in range(r0, r0 + NAT_BLOCK_ROWS):
            rs = min(max(r - kh // 2, 0), rows - kh)
            lead = rs - start
            slabs.append(jnp.concatenate([ninf(lead), strips[:, rs - r + kh - 1], ninf(span - kh - lead)], axis=-1))
        kinds.append(jnp.concatenate(slabs, axis=1))
    return jnp.stack(kinds, axis=1)


def _neighbourhood_attention(proj3, col0, cache_k4, cache_v4, layer, tiles, kh, n_heads, head_dim):
    bsz, length, _ = proj3.shape
    rows = length // GRID_W
    past = cache_k4.shape[2]
    hp = _heads_per_step(n_heads, head_dim, col0, HEADS_PER_STEP_NAT)
    bw = hp * head_dim
    nb = n_heads // hp
    base = col0 // bw
    blocks = (3 * _nbytes((length, bw), proj3.dtype) + 2 * _nbytes((past, bw), F32)
              + _nbytes((hp,) + tiles.shape[1:], F32) + _nbytes((length, bw), BF16))
    scratch = [pltpu.VMEM((hp, length, 1), F32), pltpu.VMEM((hp, length, 1), F32), pltpu.VMEM((length, bw), F32)]
    temps = (2 * hp * _nbytes((length, V7X_LANES), F32) + _nbytes((length, bw), F32)
             + 6 * hp * NAT_UNROLL * _nbytes((_tile(length, NAT_CTX_CHUNK), past), F32)
             + 6 * hp * NAT_UNROLL * _nbytes(tiles.shape[2:], F32))

    def col(g):
        return pl.BlockSpec((None, length, bw), lambda b, j: (b, 0, base + g * nb + j))

    cache_spec = pl.BlockSpec((None, None, past, bw), lambda b, j: (b, layer, 0, j))
    return pl.pallas_call(
        functools.partial(_nat_kernel, rows=rows, kh=kh, heads=hp, head_dim=head_dim),
        out_shape=jax.ShapeDtypeStruct((bsz, length, n_heads * head_dim), BF16),
        grid=(bsz, nb),
        in_specs=[col(0), col(1), col(2), cache_spec, cache_spec,
                  pl.BlockSpec((hp,) + tiles.shape[1:], lambda b, j: (j, 0, 0, 0))],
        out_specs=pl.BlockSpec((None, length, bw), lambda b, j: (b, 0, j)),
        scratch_shapes=scratch,
        compiler_params=_params(("arbitrary", "arbitrary"), blocks, temps),
        name="neighbourhood_attention",
    )(proj3, proj3, proj3, cache_k4, cache_v4, tiles)


def _dft_matrices(length):
    n = 2 * length
    split = min(DFT_SPLIT, length)
    s = np.arange(length, dtype=np.int64)
    pa = ((np.arange(length // split, dtype=np.int64)[:, None] * split * s[None, :]) % n).astype(np.int32)
    pb = ((np.arange(split, dtype=np.int64)[:, None] * s[None, :]) % n).astype(np.int32)
    ta = jnp.asarray(pa).astype(F32) * (2.0 * math.pi / n)
    tb = jnp.asarray(pb).astype(F32) * (2.0 * math.pi / n)
    ca, sa, cb, sb = jnp.cos(ta), jnp.sin(ta), jnp.cos(tb), jnp.sin(tb)
    cosm = (ca[:, None, :] * cb[None, :, :] - sa[:, None, :] * sb[None, :, :]).reshape(length, length)
    msin = -(sa[:, None, :] * cb[None, :, :] + ca[:, None, :] * sb[None, :, :]).reshape(length, length)
    k_idx = lax.broadcasted_iota(jnp.int32, (length, length), 0)
    s_idx = lax.broadcasted_iota(jnp.int32, (length, length), 1)
    sinm = jnp.where(k_idx == 0, jnp.where(s_idx % 2 == 0, 1.0, -1.0), msin)
    sinm_t = jnp.where(s_idx == 0, jnp.where(k_idx % 2 == 0, 1.0, -1.0), msin)
    return cosm.astype(BF16), sinm.astype(BF16), sinm_t.astype(BF16)


def _filter_kernel(bands_ref, w1_ref, b1_ref, f_ref, w2_ref, b2_ref, w3_ref, dl_ref, o_ref,
                   *, length, tl, emb_bands, width):
    hi = lax.Precision.HIGHEST
    pos = (lax.broadcasted_iota(jnp.int32, (tl, V7X_LANES), 0) + pl.program_id(0) * tl).astype(F32)
    lane = lax.broadcasted_iota(jnp.int32, (tl, V7X_LANES), 1)
    t = pos * (1.0 / (length - 1))
    ang = ((2.0 * math.pi / length) * pos) * bands_ref[...]
    feat = jnp.where(lane == 0, t,
                     jnp.where(lane <= emb_bands, jnp.cos(ang),
                               jnp.where(lane <= 2 * emb_bands, -jnp.sin(ang), 0.0)))
    h = jnp.sin(f_ref[0:1, :] * (jnp.dot(feat, w1_ref[...], precision=hi, preferred_element_type=F32)
                                 + b1_ref[...]))
    h = jnp.sin(f_ref[1:2, :] * (jnp.dot(h, w2_ref[...], precision=hi, preferred_element_type=F32)
                                 + b2_ref[...]))
    h = jnp.dot(h, w3_ref[...], precision=hi, preferred_element_type=F32)
    decay = jnp.exp(-(t[:, 0:1]) * dl_ref[...])
    col = lax.broadcasted_iota(jnp.int32, h.shape, 1)
    first = (pos[:, 0:1] == 0.0) & (col >= width)
    o_ref[...] = jnp.where(first, 0.0, h * decay).astype(BF16)


def _hyena_filters(length, w1, b1, freq, w2, b2, w3):
    emb, fo = w1.shape
    width = w3.shape[1] // 2
    emb_bands = (emb - 1) // 2
    tl = _tile(length, TM_DFT)
    bands = jnp.linspace(1e-4, emb_bands - 1, emb_bands, dtype=F32)
    bands_row = jnp.zeros((1, V7X_LANES), F32).at[0, 1:1 + emb_bands].set(bands)
    bands_row = bands_row.at[0, 1 + emb_bands:1 + 2 * emb_bands].set(bands)
    w1p = jnp.zeros((V7X_LANES, fo), F32).at[:emb].set(w1)
    deltas = jnp.abs(jnp.linspace(math.log(HY_TARGET) / HY_SLOW, math.log(HY_TARGET) / HY_FAST, width, dtype=F32))
    dl = jnp.concatenate([deltas, deltas]).reshape(1, 2 * width)
    full = lambda shape: pl.BlockSpec(shape, lambda i: (0,) * len(shape))
    blocks = _nbytes((tl, 2 * width), BF16) + _nbytes((fo + 2, 2 * width), F32) + _nbytes((V7X_LANES + fo, fo), F32)
    return pl.pallas_call(
        functools.partial(_filter_kernel, length=length, tl=tl, emb_bands=emb_bands, width=width),
        out_shape=jax.ShapeDtypeStruct((length, 2 * width), BF16),
        grid=(length // tl,),
        in_specs=[full((1, V7X_LANES)), full((V7X_LANES, fo)), full((1, fo)), full((2, fo)),
                  full((fo, fo)), full((1, fo)), full((fo, 2 * width)), full((1, 2 * width))],
        out_specs=pl.BlockSpec((tl, 2 * width), lambda i: (i, 0)),
        compiler_params=_params(("arbitrary",), blocks, 6 * _nbytes((tl, 2 * width), F32)),
        name="hyena_filters",
    )(bands_row, w1p, b1.reshape(1, fo), freq, w2, b2.reshape(1, fo), w3, dl)


def _spectrum_kernel(c_ref, s_ref, hf_ref, hb_ref, fr_ref, fi_ref, *, tm, norm):
    cm, sm, hf, hb = c_ref[...], s_ref[...], hf_ref[...], hb_ref[...]
    row = lax.broadcasted_iota(jnp.int32, fr_ref.shape, 0) + pl.program_id(0) * tm
    packed = row == 0
    wgt = jnp.where(packed, 0.5 * norm, norm)
    bi = _dot(sm, hb)
    fr_ref[...] = (_dot(cm, hf) + _dot(cm, hb)) * wgt
    fi_ref[...] = (_dot(sm, hf) + jnp.where(packed, bi, -bi)) * wgt


def _filter_spectrum(cosm, sinm, hcat):
    length = cosm.shape[0]
    width = hcat.shape[1] // 2
    tm, tn = _tile(length, TM_DFT), _tile(width, TN_DFT)
    nb = width // tn
    blocks = 2 * _nbytes((tm, length), BF16) + 2 * _nbytes((length, tn), BF16) + 2 * _nbytes((tm, tn), F32)
    out = jax.ShapeDtypeStruct((length, width), F32)
    ospec = pl.BlockSpec((tm, tn), lambda i, j: (i, j))
    return pl.pallas_call(
        functools.partial(_spectrum_kernel, tm=tm, norm=1.0 / length),
        out_shape=(out, out),
        grid=(length // tm, nb),
        in_specs=[pl.BlockSpec((tm, length), lambda i, j: (i, 0)),
                  pl.BlockSpec((tm, length), lambda i, j: (i, 0)),
                  pl.BlockSpec((length, tn), lambda i, j: (0, j)),
                  pl.BlockSpec((length, tn), lambda i, j: (0, nb + j))],
        out_specs=(ospec, ospec),
        compiler_params=_params(("arbitrary", "arbitrary"), blocks, 16 * _nbytes((tm, tn), F32)),
        name="filter_spectrum",
    )(cosm, sinm, hcat, hcat)


def _dft_fwd_kernel(c_ref, s_ref, z_ref, fr_ref, fi_ref, yr_ref, yi_ref, *, tm):
    z = z_ref[...]
    zr, zi = _dot(c_ref[...], z), _dot(s_ref[...], z)
    fr, fi = fr_ref[...], fi_ref[...]
    packed = (lax.broadcasted_iota(jnp.int32, zr.shape, 0) + pl.program_id(0) * tm) == 0
    yr_ref[...] = jnp.where(packed, zr * fr, zr * fr - zi * fi).astype(BF16)
    yi_ref[...] = jnp.where(packed, zi * fi, zr * fi + zi * fr).astype(BF16)


def _dft_forward(cosm, sinm, z, fr, fi):
    bsz, length, width = z.shape
    tm, tn = _tile(length, TM_DFT), _tile(width, TN_DFT_FWD)
    blocks = (2 * _nbytes((tm, length), BF16) + _nbytes((length, tn), BF16) + 2 * _nbytes((tm, tn), F32)
              + 2 * _nbytes((tm, tn), BF16))
    out = jax.ShapeDtypeStruct((bsz, length, width), BF16)
    ospec = pl.BlockSpec((None, tm, tn), lambda i, b, j: (b, i, j))
    return pl.pallas_call(
        functools.partial(_dft_fwd_kernel, tm=tm),
        out_shape=(out, out),
        grid=(length // tm, bsz, width // tn),
        in_specs=[pl.BlockSpec((tm, length), lambda i, b, j: (i, 0)),
                  pl.BlockSpec((tm, length), lambda i, b, j: (i, 0)),
                  pl.BlockSpec((None, length, tn), lambda i, b, j: (b, 0, j)),
                  pl.BlockSpec((tm, tn), lambda i, b, j: (i, j)),
                  pl.BlockSpec((tm, tn), lambda i, b, j: (i, j))],
        out_specs=(ospec, ospec),
        compiler_params=_params(("arbitrary",) * 3, blocks, 16 * _nbytes((tm, tn), F32)),
        name="dft_forward",
    )(cosm, sinm, z, fr, fi)


def _dft_inv_kernel(c_ref, st_ref, yr_ref, yi_ref, x0_ref, z_ref, b_ref, o_ref):
    y = _dot(c_ref[...], yr_ref[...]) + _dot(st_ref[...], yi_ref[...])
    y = y + z_ref[...].astype(F32) * b_ref[...]
    o_ref[...] = (x0_ref[...].astype(F32) * y).astype(BF16)


def _dft_inverse(cosm, sinm_t, yr, yi, x0, z, bias):
    bsz, length, width = z.shape
    tm, tn = _tile(length, TM_DFT), _tile(width, TN_DFT)
    blocks = (2 * _nbytes((tm, length), BF16) + 2 * _nbytes((length, tn), BF16)
              + 3 * _nbytes((tm, tn), BF16) + _nbytes((1, tn), F32))
    tile = pl.BlockSpec((None, tm, tn), lambda i, b, j: (b, i, j))
    panel = pl.BlockSpec((None, length, tn), lambda i, b, j: (b, 0, j))
    return pl.pallas_call(
        _dft_inv_kernel,
        out_shape=jax.ShapeDtypeStruct((bsz, length, width), BF16),
        grid=(length // tm, bsz, width // tn),
        in_specs=[pl.BlockSpec((tm, length), lambda i, b, j: (i, 0)),
                  pl.BlockSpec((tm, length), lambda i, b, j: (i, 0)),
                  panel, panel, tile, tile,
                  pl.BlockSpec((1, tn), lambda i, b, j: (0, j))],
        out_specs=tile,
        compiler_params=_params(("arbitrary",) * 3, blocks, 16 * _nbytes((tm, tn), F32)),
        name="dft_inverse",
    )(cosm, sinm_t, yr, yi, x0, z, bias.reshape(1, width))


def _merge_kernel(ya_ref, yb_ref, yc_ref, ga_ref, gb_ref, gc_ref, wa_ref, wb_ref, wc_ref, o_ref):
    ga = jax.nn.sigmoid(ga_ref[...].astype(F32))
    gb = jax.nn.sigmoid(gb_ref[...].astype(F32))
    gc = jax.nn.sigmoid(gc_ref[...].astype(F32))
    m = ga * _dot(ya_ref[...], wa_ref[...])
    m = m + gb * _dot(yb_ref[...], wb_ref[...])
    m = m + gc * _dot(yc_ref[...], wc_ref[...])
    o_ref[...] = m.astype(BF16)


def _merge(ya, yb, yc, proj, gate_col0, wa, wb, wc):
    t = ya.shape[0]
    d = wa.shape[1]
    tm, tn = _tile(t, TM_MM), _tile(d, TN_HALF)
    nb = d // tn
    base = gate_col0 // tn
    ka, kb, kc = ya.shape[1], yb.shape[1], yc.shape[1]
    blocks = (_nbytes((tm, ka + kb + kc), BF16) + 3 * _nbytes((tm, tn), proj.dtype)
              + _nbytes((ka + kb + kc, tn), BF16) + _nbytes((tm, tn), BF16))

    def gate(g):
        return pl.BlockSpec((tm, tn), lambda i, j: (i, base + g * nb + j))

    def panel(k):
        return pl.BlockSpec((tm, k), lambda i, j: (i, 0))

    def wcol(k):
        return pl.BlockSpec((k, tn), lambda i, j: (0, j))

    return pl.pallas_call(
        _merge_kernel,
        out_shape=jax.ShapeDtypeStruct((t, d), BF16),
        grid=(t // tm, nb),
        in_specs=[panel(ka), panel(kb), panel(kc), gate(0), gate(1), gate(2), wcol(ka), wcol(kb), wcol(kc)],
        out_specs=pl.BlockSpec((tm, tn), lambda i, j: (i, j)),
        compiler_params=_params(("arbitrary", "arbitrary"), blocks, 16 * _nbytes((tm, tn), F32)),
        name="gated_merge",
    )(ya, yb, yc, proj, proj, proj, wa, wb, wc)


def _lane_partial_sum(v):
    return functools.reduce(jnp.add, [v[:, t * V7X_LANES:(t + 1) * V7X_LANES]
                                      for t in range(v.shape[1] // V7X_LANES)])


def _mm_epilogue_kernel(a_ref, b_ref, x_ref, gpost_ref, gate_ref, *rest, n_tiles, n_chunks, by_columns,
                        with_next):
    if with_next:
        gnext_ref, sc_ref, sh_ref, xo_ref, h_ref, acc_ref = rest
    else:
        xo_ref, acc_ref = rest
    i = pl.program_id(0)
    k = pl.program_id(1)
    _, n_panels, tm, pw = acc_ref.shape
    d = n_panels * pw
    rc = tm // n_chunks
    ew = _tile(pw, EPI_COL_PIECE)
    chunk = jnp.minimum(k, n_chunks - 1)

    @pl.when((i == 0) & (k == 0))
    def _():
        acc_ref[...] = jnp.zeros_like(acc_ref)

    def accumulate_stages(slot):
        def whole():
            acc_ref[slot, k] = _dot(a_ref[...], b_ref[...]).astype(acc_ref.dtype)

        def panel(c):
            part = _dot(a_ref[...], b_ref[:, c * pw:(c + 1) * pw])
            acc_ref[slot, c] = jnp.where(k == 0, part, acc_ref[slot, c] + part)

        return [whole] if by_columns else [functools.partial(panel, c) for c in range(n_panels)]

    def epilogue_stages(slot):
        rows = pl.ds(pl.multiple_of(chunk * rc, rc), rc)
        pieces = [(c, slice(e * ew, (e + 1) * ew), slice(c * pw + e * ew, c * pw + (e + 1) * ew))
                  for c in range(n_panels) for e in range(pw // ew)]
        rstd = {}

        def scale_of(ssq):
            return lax.rsqrt(jnp.sum(ssq, axis=-1, keepdims=True) / d + RMS_EPS)

        def product_norm():
            rstd["post"] = scale_of(functools.reduce(
                jnp.add, [_lane_partial_sum(jnp.square(acc_ref[slot, c, rows, ps].astype(F32)))
                          for c, ps, _ in pieces]))

        def residual():
            ssq = jnp.zeros((rc, V7X_LANES), F32)
            for c, ps, cs in pieces:
                y = acc_ref[slot, c, rows, ps].astype(F32) * rstd["post"] * gpost_ref[:, cs]
                xn = x_ref[:, cs] + gate_ref[:, cs] * y
                xo_ref[:, cs] = xn
                ssq = ssq + _lane_partial_sum(jnp.square(xn))
            rstd["next"] = scale_of(ssq)

        def next_prenorm():
            for _, _, cs in pieces:
                hn = xo_ref[:, cs] * rstd["next"] * gnext_ref[:, cs]
                h_ref[:, cs] = (hn * (1.0 + sc_ref[:, cs]) + sh_ref[:, cs]).astype(BF16)

        return [product_norm, residual] + ([next_prenorm] if with_next else [])

    def run(*stage_lists):
        for stage in itertools.chain(*stage_lists):
            stage()

    @pl.when(i == 0)
    def _():
        run(accumulate_stages(0))

    inner = (i > 0) & (i < n_tiles)

    @pl.when(inner & (i % 2 == 0))
    def _():
        run(epilogue_stages(1), accumulate_stages(0))

    @pl.when(inner & (i % 2 == 1))
    def _():
        run(epilogue_stages(0), accumulate_stages(1))

    @pl.when(i == n_tiles)
    def _():
        run(epilogue_stages((n_tiles - 1) % 2))


def _matmul_epilogue(a, b, x2, gain_post, mod_l, gate_idx, rows_per_batch, nxt):
    t, kdim = a.shape
    d = b.shape[1]
    by_columns = kdim <= EPI_FULL_DEPTH_MAX
    acc_dtype = BF16 if by_columns else F32
    tm, tpb = _row_tiling(mod_l, t, rows_per_batch, TM_EPI_FULL_DEPTH if by_columns else TM_EPI)
    n_tiles = t // tm
    if by_columns:
        pw = _tile(d, EPI_COL_PIECE)
        steps = d // pw
        a_spec = pl.BlockSpec((tm, kdim), lambda i, k: (jnp.minimum(i, n_tiles - 1), 0))
        b_spec = pl.BlockSpec((kdim, pw), lambda i, k: (0, jnp.where(i == n_tiles, steps - 1, k)))
        blocks = _nbytes((tm, kdim), BF16) + _nbytes((kdim, pw), BF16)
    else:
        fits = [w for w in EPI_K_TILES if kdim % w == 0]
        tk = next((w for w in fits if kdim // w >= EPI_ROW_CHUNKS), fits[-1])
        pw = _tile(d, EPI_COL_CHUNK)
        steps = kdim // tk
        a_spec = pl.BlockSpec((tm, tk), lambda i, k: (jnp.minimum(i, n_tiles - 1),
                                                      jnp.where(i == n_tiles, steps - 1, k)))
        b_spec = pl.BlockSpec((tk, d), lambda i, k: (jnp.where(i == n_tiles, steps - 1, k), 0))
        blocks = _nbytes((tm, tk), BF16) + _nbytes((tk, d), BF16)
    n_chunks = max(n for n in range(1, min(steps, EPI_ROW_CHUNKS) + 1) if tm % (n * 2 * V7X_SUBLANES) == 0)
    rc = tm // n_chunks

    def prev_tile(i):
        return jnp.maximum(i - 1, 0)

    def chunk_index(i, k):
        return jnp.where(i == 0, 0, (i - 1) * n_chunks + jnp.minimum(k, n_chunks - 1))

    def vec(which):
        return pl.BlockSpec((None, None, 1, d), lambda i, k: (prev_tile(i) // tpb, which, 0, 0))

    row = pl.BlockSpec((1, d), lambda i, k: (0, 0))
    chunk_rows = pl.BlockSpec((rc, d), lambda i, k: (chunk_index(i, k), 0))
    in_specs = [a_spec, b_spec, chunk_rows, row, vec(gate_idx)]
    args = [a, b, x2, gain_post.reshape(1, d), mod_l]
    blocks += 2 * _nbytes((rc, d), F32) + 5 * _nbytes((1, d), F32)
    if nxt is None:
        out_shape = jax.ShapeDtypeStruct((t, d), F32)
        out_specs = chunk_rows
    else:
        gain_next, mod_next, sc_idx, sh_idx = nxt
        in_specs += [row, vec(sc_idx), vec(sh_idx)]
        args += [gain_next.reshape(1, d), mod_next, mod_next]
        out_shape = (jax.ShapeDtypeStruct((t, d), F32), jax.ShapeDtypeStruct((t, d), BF16))
        out_specs = (chunk_rows, chunk_rows)
        blocks += _nbytes((rc, d), BF16)
    temps = (_nbytes((2, tm, d), acc_dtype) + 3 * _nbytes((tm, pw), F32)
             + 8 * _nbytes((rc, _tile(pw, EPI_COL_PIECE)), F32))
    return pl.pallas_call(
        functools.partial(_mm_epilogue_kernel, n_tiles=n_tiles, n_chunks=n_chunks, by_columns=by_columns,
                          with_next=nxt is not None),
        out_shape=out_shape,
        grid=(n_tiles + 1, steps),
        in_specs=in_specs,
        out_specs=out_specs,
        scratch_shapes=[pltpu.VMEM((2, d // pw, tm, pw), acc_dtype)],
        compiler_params=_params(("arbitrary", "arbitrary"), blocks, temps),
        name="matmul_norm_residual",
    )(*args)


def _ffn_up_kernel(h_ref, wg_ref, wu_ref, o_ref):
    h = h_ref[...]
    o_ref[...] = (jax.nn.silu(_dot(h, wg_ref[...])) * _dot(h, wu_ref[...])).astype(BF16)


def _ffn_up(h, wg, wu):
    t, d = h.shape
    n = wg.shape[1]
    tm, tn = _tile(t, TM_MM), _tile(n, TN_HALF)
    blocks = _nbytes((tm, d), BF16) + 2 * _nbytes((d, tn), BF16) + _nbytes((tm, tn), BF16)
    return pl.pallas_call(
        _ffn_up_kernel,
        out_shape=jax.ShapeDtypeStruct((t, n), BF16),
        grid=(t // tm, n // tn),
        in_specs=[pl.BlockSpec((tm, d), lambda i, j: (i, 0)),
                  pl.BlockSpec((d, tn), lambda i, j: (0, j)),
                  pl.BlockSpec((d, tn), lambda i, j: (0, j))],
        out_specs=pl.BlockSpec((tm, tn), lambda i, j: (i, j)),
        compiler_params=_params(("arbitrary", "arbitrary"), blocks, 16 * _nbytes((tm, tn), F32)),
        name="ffn_up",
    )(h, wg, wu)


def _pad_cols(w, mult):
    pad = (-w.shape[-1]) % mult
    return jnp.pad(w, ((0, 0), (0, 0), (0, pad))) if pad else w


def _pad_rows(w, mult):
    pad = (-w.shape[-2]) % mult
    return jnp.pad(w, ((0, 0), (0, pad), (0, 0))) if pad else w


def kernel(x_prompt, x_sample, cache_k, cache_v, c, c_ctx, w_mod, b_mod, norm_gains, w_in, conv_a, w_up_a, na_rpb, w_up_b, conv_c, filt_w1, filt_b1, filt_freq, filt_w2, filt_b2, filt_w3, hyena_bias, w_up_c, w_out, w_ffn_gate, w_ffn_up, w_ffn_down):
    depth, d, _ = w_mod.shape
    n_heads, head_dim = cache_k.shape[3], cache_k.shape[4]
    sc_w = conv_a.shape[-1]
    na_w = n_heads * head_dim
    hy_w = hyena_bias.shape[-1]
    kh_full = (na_rpb.shape[2] + 1) // 2
    kw = (na_rpb.shape[3] + 1) // 2
    col_na = 3 * sc_w
    col_hy = col_na + 3 * na_w
    col_gate = col_hy + 3 * hy_w
    assert x_sample.shape[1] % GRID_W == 0 and x_sample.shape[1] // GRID_W >= kh_full
    assert 1 + c.shape[0] <= MOD_ROWS_PAD

    w_in_b = w_in.astype(BF16)
    w_up_a_b, w_up_b_b, w_up_c_b = w_up_a.astype(BF16), w_up_b.astype(BF16), w_up_c.astype(BF16)
    w_out_b = w_out.astype(BF16)
    w_gate_b = _pad_cols(w_ffn_gate.astype(BF16), FFN_PAD)
    w_upf_b = _pad_cols(w_ffn_up.astype(BF16), FFN_PAD)
    w_down_b = _pad_rows(w_ffn_down.astype(BF16), FFN_PAD)

    c_rows = jnp.zeros((MOD_ROWS_PAD, d), F32).at[0].set(c_ctx).at[1:1 + c.shape[0]].set(c)
    mod = _modulation(c_rows, w_mod, b_mod).reshape(depth, MOD_ROWS_PAD, N_MOD, 1, d)
    cache_k4 = cache_k.reshape(cache_k.shape[:3] + (na_w,))
    cache_v4 = cache_v.reshape(cache_v.shape[:3] + (na_w,))

    def run_group(x3, mod_g, latent):
        bsz, length, _ = x3.shape
        proj_dtype = BF16 if latent else F32
        cosm, sinm, sinm_t = _dft_matrices(length)
        x2 = x3.reshape(bsz * length, d)
        h = _prenorm(x2, norm_gains[0, 0], mod_g[0], length)
        ks, vs = [], []
        for l in range(depth):
            proj = _matmul(h, w_in_b[l], proj_dtype)
            proj3 = proj.reshape(bsz, length, proj.shape[1])
            y_sc = _short_conv(proj3, conv_a[l], 0)
            if latent:
                tiles = _nat_block_tiles(_nat_bias_strips(na_rpb[l], kh_full, kw), length // GRID_W)
                y_na = _neighbourhood_attention(proj3, col_na, cache_k4, cache_v4, l, tiles, kh_full,
                                                n_heads, head_dim)
            else:
                y_na = _context_attention(proj3, col_na, n_heads, head_dim)
                ks.append(proj3[:, :, col_na + na_w:col_na + 2 * na_w].reshape(bsz, length, n_heads, head_dim))
                vs.append(proj3[:, :, col_na + 2 * na_w:col_hy].reshape(bsz, length, n_heads, head_dim))
            x0, z = _hyena_pre(proj3, conv_c[l], col_hy)
            hcat = _hyena_filters(length, filt_w1[l], filt_b1[l], filt_freq[l], filt_w2[l], filt_b2[l], filt_w3[l])
            fr, fi = _filter_spectrum(cosm, sinm, hcat)
            yr, yi = _dft_forward(cosm, sinm, z, fr, fi)
            y_hy = _dft_inverse(cosm, sinm_t, yr, yi, x0, z, hyena_bias[l])
            merged = _merge(y_sc.reshape(-1, sc_w), y_na.reshape(-1, na_w), y_hy.reshape(-1, hy_w),
                            proj, col_gate, w_up_a_b[l], w_up_b_b[l], w_up_c_b[l])
            x2, h2 = _matmul_epilogue(merged, w_out_b[l], x2, norm_gains[l, 1], mod_g[l], 2, length,
                                      (norm_gains[l, 2], mod_g[l], 4, 3))
            hidden = _ffn_up(h2, w_gate_b[l], w_upf_b[l])
            if l + 1 < depth:
                x2, h = _matmul_epilogue(hidden, w_down_b[l], x2, norm_gains[l, 3], mod_g[l], 5, length,
                                         (norm_gains[l + 1, 0], mod_g[l + 1], 1, 0))
            else:
                x2 = _matmul_epilogue(hidden, w_down_b[l], x2, norm_gains[l, 3], mod_g[l], 5, length, None)
        return x2.reshape(bsz, length, d), ks, vs

    y_prompt, ks, vs = run_group(x_prompt, mod[:, 0:1], latent=False)
    y_sample, _, _ = run_group(x_sample, mod[:, 1:1 + c.shape[0]], latent=True)
    return (y_prompt, y_sample, jnp.stack(ks, axis=1), jnp.stack(vs, axis=1))
```

```python
import functools
import itertools
import math

import numpy as np
import jax
import jax.numpy as jnp
from jax import lax
from jax.experimental import pallas as pl
from jax.experimental.pallas import tpu as pltpu

F32 = jnp.float32
BF16 = jnp.bfloat16

GRID_W = 64
HY_FAST = 0.3
HY_SLOW = 1.5
HY_TARGET = 1e-2
RMS_EPS = 1e-6
N_MOD = 6

V7X_VMEM_BYTES = 64 * 1024 * 1024
V7X_VMEM_RESERVED_BYTES = 6 * 1024 * 1024
V7X_LANES = 128
V7X_SUBLANES = 8
MOD_ROWS_PAD = 16

TM_MM = 1024
TN_MM = 1024
TN_HALF = 512
TM_EPI = 512
TM_EPI_FULL_DEPTH = 1024
EPI_K_TILES = (1024, 512, 256, 128)
EPI_FULL_DEPTH_MAX = 4096
EPI_ROW_CHUNKS = 8
EPI_COL_CHUNK = 1024
EPI_COL_PIECE = 512
TM_DFT = 512
TN_DFT = 512
TN_DFT_FWD = 1024
CONV_TILE_ELEMS = 1024 * 1024
FFN_PAD = 1024
HEADS_PER_STEP_CTX = 8
HEADS_PER_STEP_NAT = 2
NAT_CTX_CHUNK = 512
NAT_BLOCK_ROWS = 4
NAT_UNROLL = 4
DFT_SPLIT = 64


def _tile(dim, pref):
    return pref if dim % pref == 0 else dim


def _nbytes(shape, dtype):
    return int(np.prod(shape)) * jnp.dtype(dtype).itemsize


def _params(semantics, block_bytes, temp_bytes=0):
    need = 2 * block_bytes + temp_bytes
    limit = min(V7X_VMEM_BYTES - V7X_VMEM_RESERVED_BYTES, max(need, 16 * 1024 * 1024))
    return pltpu.CompilerParams(dimension_semantics=semantics, vmem_limit_bytes=limit)


def _dot(a, b):
    return jnp.dot(a, b, preferred_element_type=F32)


def _dot_t(a, b):
    return lax.dot_general(a, b, (((1,), (1,)), ((), ())), preferred_element_type=F32)


def _rms(x, gain):
    return x * lax.rsqrt(jnp.mean(x * x, axis=-1, keepdims=True) + RMS_EPS) * gain


def _mod_kernel(c_ref, w_ref, b_ref, o_ref):
    a = jax.nn.silu(c_ref[...]).astype(BF16)
    o_ref[...] = _dot(a, w_ref[...].astype(BF16)) + b_ref[...]


def _modulation(c_rows, w_mod, b_mod):
    depth, d, n = w_mod.shape
    tn = _tile(n, TN_HALF)
    blocks = _nbytes((MOD_ROWS_PAD, d), F32) + _nbytes((d, tn), F32) + _nbytes((MOD_ROWS_PAD + 1, tn), F32)
    return pl.pallas_call(
        _mod_kernel,
        out_shape=jax.ShapeDtypeStruct((depth, MOD_ROWS_PAD, n), F32),
        grid=(depth, n // tn),
        in_specs=[pl.BlockSpec((MOD_ROWS_PAD, d), lambda l, j: (0, 0)),
                  pl.BlockSpec((None, d, tn), lambda l, j: (l, 0, j)),
                  pl.BlockSpec((None, 1, tn), lambda l, j: (l, 0, j))],
        out_specs=pl.BlockSpec((None, MOD_ROWS_PAD, tn), lambda l, j: (l, 0, j)),
        compiler_params=_params(("arbitrary", "arbitrary"), blocks, _nbytes((d, tn), BF16)),
        name="modulation",
    )(c_rows, w_mod, b_mod.reshape(depth, 1, n))


def _row_tiling(mod_l, total_rows, rows_per_batch, pref):
    shared = mod_l.shape[0] == 1
    tm = _tile(total_rows if shared else rows_per_batch, pref)
    return tm, (total_rows if shared else rows_per_batch) // tm


def _vec_spec(d, which, tiles_per_batch):
    return pl.BlockSpec((None, None, 1, d), lambda i: (i // tiles_per_batch, which, 0, 0))


def _prenorm_kernel(x_ref, g_ref, sc_ref, sh_ref, h_ref):
    y = _rms(x_ref[...], g_ref[...])
    h_ref[...] = (y * (1.0 + sc_ref[...]) + sh_ref[...]).astype(BF16)


def _prenorm(x2, gain, mod_l, rows_per_batch):
    t, d = x2.shape
    tm, tpb = _row_tiling(mod_l, t, rows_per_batch, TM_EPI)
    blocks = _nbytes((tm, d), F32) + _nbytes((tm, d), BF16) + 3 * _nbytes((1, d), F32)
    return pl.pallas_call(
        _prenorm_kernel,
        out_shape=jax.ShapeDtypeStruct((t, d), BF16),
        grid=(t // tm,),
        in_specs=[pl.BlockSpec((tm, d), lambda i: (i, 0)),
                  pl.BlockSpec((1, d), lambda i: (0, 0)),
                  _vec_spec(d, 1, tpb),
                  _vec_spec(d, 0, tpb)],
        out_specs=pl.BlockSpec((tm, d), lambda i: (i, 0)),
        compiler_params=_params(("arbitrary",), blocks, 2 * _nbytes((tm, d), F32)),
        name="prenorm",
    )(x2, gain.reshape(1, d), mod_l, mod_l)


def _mm_kernel(a_ref, b_ref, o_ref):
    o_ref[...] = _dot(a_ref[...], b_ref[...]).astype(o_ref.dtype)


def _matmul(a, b, layer, out_dtype):
    m, k = a.shape
    n = b.shape[2]
    tm, tn = _tile(m, TM_MM), _tile(n, TN_MM)
    blocks = _nbytes((tm, k), BF16) + _nbytes((k, tn), BF16) + _nbytes((tm, tn), out_dtype)
    return pl.pallas_call(
        _mm_kernel,
        out_shape=jax.ShapeDtypeStruct((m, n), out_dtype),
        grid=(m // tm, n // tn),
        in_specs=[pl.BlockSpec((tm, k), lambda i, j: (i, 0)),
                  pl.BlockSpec((None, k, tn), lambda i, j: (layer, 0, j))],
        out_specs=pl.BlockSpec((tm, tn), lambda i, j: (i, j)),
        compiler_params=_params(("arbitrary", "arbitrary"), blocks, 3 * _nbytes((tm, tn), F32)),
        name="in_proj",
    )(a, b)


def _dwconv3(s, w):
    n = s.shape[0]
    row = lax.broadcasted_iota(jnp.int32, s.shape, 0)
    prev = jnp.where(row == 0, 0.0, pltpu.roll(s, 1, 0))
    nxt = jnp.where(row == n - 1, 0.0, pltpu.roll(s, n - 1, 0))
    return prev * w[0:1, :] + s * w[1:2, :] + nxt * w[2:3, :]


def _sconv_kernel(b_ref, c_ref, x_ref, w_ref, o_ref):
    s = c_ref[...].astype(F32) * x_ref[...].astype(F32)
    o_ref[...] = (b_ref[...].astype(F32) * _dwconv3(s, w_ref[...])).astype(BF16)


def _short_conv(proj3, conv_w, col0):
    bsz, length, _ = proj3.shape
    width = conv_w.shape[1]
    tc = _tile(width, max(V7X_LANES, CONV_TILE_ELEMS // length))
    nb = width // tc
    base = col0 // tc
    blocks = 3 * _nbytes((length, tc), proj3.dtype) + _nbytes((3, tc), F32) + _nbytes((length, tc), BF16)

    def col(g):
        return pl.BlockSpec((None, length, tc), lambda b, j: (b, 0, base + g * nb + j))

    return pl.pallas_call(
        _sconv_kernel,
        out_shape=jax.ShapeDtypeStruct((bsz, length, width), BF16),
        grid=(bsz, nb),
        in_specs=[col(0), col(1), col(2), pl.BlockSpec((3, tc), lambda b, j: (0, j))],
        out_specs=pl.BlockSpec((None, length, tc), lambda b, j: (b, 0, j)),
        compiler_params=_params(("arbitrary", "arbitrary"), blocks, 6 * _nbytes((length, tc), F32)),
        name="short_conv",
    )(proj3, proj3, proj3, conv_w)


def _hyena_pre_kernel(p0_ref, p1_ref, p2_ref, w0_ref, w1_ref, w2_ref, x0_ref, z_ref):
    x0_ref[...] = _dwconv3(p0_ref[...].astype(F32), w0_ref[...]).astype(BF16)
    x1 = _dwconv3(p1_ref[...].astype(F32), w1_ref[...])
    v = _dwconv3(p2_ref[...].astype(F32), w2_ref[...])
    z_ref[...] = (x1 * v).astype(BF16)


def _hyena_pre(proj3, conv_w, col0):
    bsz, length, _ = proj3.shape
    width = conv_w.shape[1] // 3
    tc = _tile(width, max(V7X_LANES, CONV_TILE_ELEMS // length))
    nb = width // tc
    base = col0 // tc
    blocks = (3 * _nbytes((length, tc), proj3.dtype) + 3 * _nbytes((3, tc), F32)
              + 2 * _nbytes((length, tc), BF16))

    def col(g):
        return pl.BlockSpec((None, length, tc), lambda b, j: (b, 0, base + g * nb + j))

    def wcol(g):
        return pl.BlockSpec((3, tc), lambda b, j: (0, g * nb + j))

    out = jax.ShapeDtypeStruct((bsz, length, width), BF16)
    ospec = pl.BlockSpec((None, length, tc), lambda b, j: (b, 0, j))
    return pl.pallas_call(
        _hyena_pre_kernel,
        out_shape=(out, out),
        grid=(bsz, nb),
        in_specs=[col(0), col(1), col(2), wcol(0), wcol(1), wcol(2)],
        out_specs=(ospec, ospec),
        compiler_params=_params(("arbitrary", "arbitrary"), blocks, 8 * _nbytes((length, tc), F32)),
        name="hyena_pre",
    )(proj3, proj3, proj3, conv_w, conv_w, conv_w)


def _softmax_pv(parts):
    m = functools.reduce(jnp.maximum, [jnp.max(s, axis=-1, keepdims=True) for s, _ in parts])
    ps = [jnp.exp(s - m) for s, _ in parts]
    denom = functools.reduce(jnp.add, [jnp.sum(p, axis=-1, keepdims=True) for p in ps])
    acc = functools.reduce(jnp.add, [_dot(p.astype(BF16), v) for p, (_, v) in zip(ps, parts)])
    return acc / denom


def _ctx_attn_kernel(q_ref, k_ref, v_ref, o_ref, *cache_refs, heads, head_dim):
    scale = head_dim ** -0.5
    for h in range(heads):
        sl = slice(h * head_dim, (h + 1) * head_dim)
        q = q_ref[:, sl].astype(BF16)
        k = k_ref[:, sl].astype(BF16)
        v = v_ref[:, sl].astype(BF16)
        o_ref[:, sl] = _softmax_pv([(_dot_t(q, k) * scale, v)]).astype(BF16)
        if cache_refs:
            ko_ref, vo_ref = cache_refs
            ko_ref[:, h, :] = k_ref[:, sl]
            vo_ref[:, h, :] = v_ref[:, sl]


def _context_attention(proj3, col0, n_heads, head_dim):
    bsz, length, _ = proj3.shape
    hp = _heads_per_step(n_heads, head_dim, col0, HEADS_PER_STEP_CTX)
    bw = hp * head_dim
    nb = n_heads // hp
    base = col0 // bw
    na_w = n_heads * head_dim
    blocks = 3 * _nbytes((length, bw), proj3.dtype) + _nbytes((length, bw), BF16)
    y_shape = jax.ShapeDtypeStruct((bsz, length, na_w), BF16)
    y_spec = pl.BlockSpec((None, length, bw), lambda b, j: (b, 0, j))
    in_kernel_cache = hp % V7X_SUBLANES == 0 or hp == n_heads

    def col(g):
        return pl.BlockSpec((None, length, bw), lambda b, j: (b, 0, base + g * nb + j))

    if in_kernel_cache:
        kv_shape = jax.ShapeDtypeStruct((bsz, length, n_heads, head_dim), proj3.dtype)
        kv_spec = pl.BlockSpec((None, length, hp, head_dim), lambda b, j: (b, 0, j, 0))
        out_shape, out_specs = (y_shape, kv_shape, kv_shape), (y_spec, kv_spec, kv_spec)
        blocks += 2 * _nbytes((length, bw), proj3.dtype)
    else:
        out_shape, out_specs = y_shape, y_spec
    out = pl.pallas_call(
        functools.partial(_ctx_attn_kernel, heads=hp, head_dim=head_dim),
        out_shape=out_shape,
        grid=(bsz, nb),
        in_specs=[col(0), col(1), col(2)],
        out_specs=out_specs,
        compiler_params=_params(("arbitrary", "arbitrary"), blocks, 8 * _nbytes((length, length), F32)),
        name="context_attention",
    )(proj3, proj3, proj3)
    if in_kernel_cache:
        return out
    k = proj3[:, :, col0 + na_w:col0 + 2 * na_w].reshape(bsz, length, n_heads, head_dim)
    v = proj3[:, :, col0 + 2 * na_w:col0 + 3 * na_w].reshape(bsz, length, n_heads, head_dim)
    return out, k, v


def _heads_per_step(n_heads, head_dim, col0, cap):
    return max(h for h in range(1, cap + 1) if n_heads % h == 0 and col0 % (h * head_dim) == 0)


def _nat_kernel(q_ref, k_ref, v_ref, kc_ref, vc_ref, bias_ref, o_ref, mc_ref, lc_ref, oc_ref,
                *, rows, kh, heads, head_dim):
    scale = head_dim ** -0.5
    length = rows * GRID_W
    past = kc_ref.shape[0]
    cch = _tile(length, NAT_CTX_CHUNK)
    n_cch = length // cch
    c_unroll = NAT_UNROLL if n_cch % NAT_UNROLL == 0 else 1
    span = NAT_BLOCK_ROWS + kh
    n_blk = rows // NAT_BLOCK_ROWS
    b_unroll = NAT_UNROLL if n_blk % NAT_UNROLL == 0 else 1
    qn, kn = NAT_BLOCK_ROWS * GRID_W, span * GRID_W
    lanes = [slice(h * head_dim, (h + 1) * head_dim) for h in range(heads)]

    def block_diag(parts):
        zero = jnp.zeros_like(parts[0])
        return jnp.concatenate([jnp.concatenate([p if j == i else zero for j in range(heads)], axis=-1)
                                for i, p in enumerate(parts)], axis=0)

    kc = block_diag([kc_ref[:, ls].astype(BF16) for ls in lanes])
    vc = block_diag([vc_ref[:, ls].astype(BF16) for ls in lanes])

    def ctx_group(g, carry):
        rws = [pl.ds(pl.multiple_of((g * c_unroll + u) * cch, cch), cch) for u in range(c_unroll)]
        scores = [_dot_t(q_ref[rw, :], kc) * scale for rw in rws]
        probs = []
        for rw, s in zip(rws, scores):
            per_head = []
            for h in range(heads):
                sh = s[:, h * past:(h + 1) * past]
                m = jnp.max(sh, axis=-1, keepdims=True)
                p = jnp.exp(sh - m)
                mc_ref[h, rw, :] = m
                lc_ref[h, rw, :] = jnp.sum(p, axis=-1, keepdims=True)
                per_head.append(p.astype(BF16))
            probs.append(jnp.concatenate(per_head, axis=-1))
        for rw, p in zip(rws, probs):
            oc_ref[rw, :] = _dot(p, vc)
        return carry

    lax.fori_loop(0, n_cch // c_unroll, ctx_group, 0)

    def block_group(g, carry):
        qrows, wins, kinds = [], [], []
        for u in range(b_unroll):
            blk = g * b_unroll + u
            r0 = blk * NAT_BLOCK_ROWS
            start = jnp.clip(r0 - kh // 2, 0, rows - span)
            qrows.append(pl.ds(pl.multiple_of(r0 * GRID_W, qn), qn))
            wins.append(pl.ds(pl.multiple_of(start * GRID_W, GRID_W), kn))
            kinds.append(jnp.where(blk == 0, 0, jnp.where(blk == n_blk - 1, 2, 1)))
        scores = [_dot_t(q_ref[qr, :], block_diag([k_ref[w, ls] for ls in lanes])) * scale
                  for qr, w in zip(qrows, wins)]
        probs, wcs, denoms = [], [], []
        for qr, s, kd in zip(qrows, scores, kinds):
            per_head, wc_h, denom_h = [], [], []
            for h in range(heads):
                sh = s[:, h * kn:(h + 1) * kn] + bias_ref[h, kd]
                mc = mc_ref[h, qr, :]
                m = jnp.maximum(jnp.max(sh, axis=-1, keepdims=True), mc)
                p = jnp.exp(sh - m)
                wc = jnp.exp(mc - m)
                per_head.append(p.astype(BF16))
                wc_h.append(wc)
                denom_h.append(jnp.sum(p, axis=-1, keepdims=True) + lc_ref[h, qr, :] * wc)
            probs.append(jnp.concatenate(per_head, axis=-1))
            wcs.append(wc_h)
            denoms.append(denom_h)
        accs = [_dot(p, block_diag([v_ref[w, ls] for ls in lanes])) for p, w in zip(probs, wins)]
        for qr, acc, wc_h, denom_h in zip(qrows, accs, wcs, denoms):
            for h, ls in enumerate(lanes):
                o_ref[qr, ls] = ((acc[:, ls] + oc_ref[qr, ls] * wc_h[h]) / denom_h[h]).astype(BF16)
        return carry

    lax.fori_loop(0, n_blk // b_unroll, block_group, 0)


def _nat_bias_strips(rpb, kh, kw):
    n_heads = rpb.shape[0]
    qc = np.arange(GRID_W)[:, None]
    kc = np.arange(GRID_W)[None, :]
    cstart = np.clip(qc - kw // 2, 0, GRID_W - kw)
    ok = (kc >= cstart) & (kc < cstart + kw)
    padded = jnp.pad(rpb, ((0, 0), (0, 0), (GRID_W - kw, GRID_W - kw)))
    toep = jnp.stack([padded[:, :, GRID_W - 1 - q:2 * GRID_W - 1 - q] for q in range(GRID_W)], axis=2)
    toep = jnp.where(ok[None, None], toep, -jnp.inf)
    return jnp.stack([jnp.transpose(toep[:, o:o + kh], (0, 2, 1, 3)).reshape(n_heads, GRID_W, kh * GRID_W)
                      for o in range(kh)], axis=1)


def _nat_block_tiles(strips, rows):
    n_heads, kh = strips.shape[:2]
    span = NAT_BLOCK_ROWS + kh
    n_blk = rows // NAT_BLOCK_ROWS
    assert rows % NAT_BLOCK_ROWS == 0 and rows >= span and NAT_BLOCK_ROWS >= kh // 2

    def ninf(width):
        return jnp.full((n_heads, GRID_W, width * GRID_W), -jnp.inf, F32)

    kinds = []
    for blk in (0, min(1, n_blk - 1), n_blk - 1):
        r0 = blk * NAT_BLOCK_ROWS
        start = min(max(r0 - kh // 2, 0), rows - span)
        slabs = []
        for r in range(r0, r0 + NAT_BLOCK_ROWS):
            rs = min(max(r - kh // 2, 0), rows - kh)
            lead = rs - start
            slabs.append(jnp.concatenate([ninf(lead), strips[:, rs - r + kh - 1], ninf(span - kh - lead)], axis=-1))
        kinds.append(jnp.concatenate(slabs, axis=1))
    return jnp.stack(kinds, axis=1)


def _neighbourhood_attention(proj3, col0, cache_k4, cache_v4, layer, tiles, kh, n_heads, head_dim):
    bsz, length, _ = proj3.shape
    rows = length // GRID_W
    past = cache_k4.shape[2]
    hp = _heads_per_step(n_heads, head_dim, col0, HEADS_PER_STEP_NAT)
    bw = hp * head_dim
    nb = n_heads // hp
    base = col0 // bw
    blocks = (3 * _nbytes((length, bw), proj3.dtype) + 2 * _nbytes((past, bw), F32)
              + _nbytes((hp,) + tiles.shape[1:], F32) + _nbytes((length, bw), BF16))
    scratch = [pltpu.VMEM((hp, length, 1), F32), pltpu.VMEM((hp, length, 1), F32), pltpu.VMEM((length, bw), F32)]
    temps = (2 * hp * _nbytes((length, V7X_LANES), F32) + _nbytes((length, bw), F32)
             + 6 * hp * NAT_UNROLL * _nbytes((_tile(length, NAT_CTX_CHUNK), past), F32)
             + 6 * hp * NAT_UNROLL * _nbytes(tiles.shape[2:], F32))

    def col(g):
        return pl.BlockSpec((None, length, bw), lambda b, j: (b, 0, base + g * nb + j))

    cache_spec = pl.BlockSpec((None, None, past, bw), lambda b, j: (b, layer, 0, j))
    return pl.pallas_call(
        functools.partial(_nat_kernel, rows=rows, kh=kh, heads=hp, head_dim=head_dim),
        out_shape=jax.ShapeDtypeStruct((bsz, length, n_heads * head_dim), BF16),
        grid=(bsz, nb),
        in_specs=[col(0), col(1), col(2), cache_spec, cache_spec,
                  pl.BlockSpec((hp,) + tiles.shape[1:], lambda b, j: (j, 0, 0, 0))],
        out_specs=pl.BlockSpec((None, length, bw), lambda b, j: (b, 0, j)),
        scratch_shapes=scratch,
        compiler_params=_params(("arbitrary", "arbitrary"), blocks, temps),
        name="neighbourhood_attention",
    )(proj3, proj3, proj3, cache_k4, cache_v4, tiles)


def _dft_matrices(length):
    n = 2 * length
    split = min(DFT_SPLIT, length)
    s = np.arange(length, dtype=np.int64)
    pa = ((np.arange(length // split, dtype=np.int64)[:, None] * split * s[None, :]) % n).astype(np.int32)
    pb = ((np.arange(split, dtype=np.int64)[:, None] * s[None, :]) % n).astype(np.int32)
    ta = jnp.asarray(pa).astype(F32) * (2.0 * math.pi / n)
    tb = jnp.asarray(pb).astype(F32) * (2.0 * math.pi / n)
    ca, sa, cb, sb = jnp.cos(ta), jnp.sin(ta), jnp.cos(tb), jnp.sin(tb)
    cosm = (ca[:, None, :] * cb[None, :, :] - sa[:, None, :] * sb[None, :, :]).reshape(length, length)
    msin = -(sa[:, None, :] * cb[None, :, :] + ca[:, None, :] * sb[None, :, :]).reshape(length, length)
    k_idx = lax.broadcasted_iota(jnp.int32, (length, length), 0)
    s_idx = lax.broadcasted_iota(jnp.int32, (length, length), 1)
    sinm = jnp.where(k_idx == 0, jnp.where(s_idx % 2 == 0, 1.0, -1.0), msin)
    sinm_t = jnp.where(s_idx == 0, jnp.where(k_idx % 2 == 0, 1.0, -1.0), msin)
    return cosm.astype(BF16), sinm.astype(BF16), sinm_t.astype(BF16)


def _filter_kernel(bands_ref, w1_ref, b1_ref, f_ref, w2_ref, b2_ref, w3_ref, dl_ref, o_ref,
                   *, length, tl, emb_bands, width):
    hi = lax.Precision.HIGHEST
    pos = (lax.broadcasted_iota(jnp.int32, (tl, V7X_LANES), 0) + pl.program_id(0) * tl).astype(F32)
    lane = lax.broadcasted_iota(jnp.int32, (tl, V7X_LANES), 1)
    t = pos * (1.0 / (length - 1))
    ang = ((2.0 * math.pi / length) * pos) * bands_ref[...]
    feat = jnp.where(lane == 0, t,
                     jnp.where(lane <= emb_bands, jnp.cos(ang),
                               jnp.where(lane <= 2 * emb_bands, -jnp.sin(ang), 0.0)))
    h = jnp.sin(f_ref[0:1, :] * (jnp.dot(feat, w1_ref[...], precision=hi, preferred_element_type=F32)
                                 + b1_ref[...]))
    h = jnp.sin(f_ref[1:2, :] * (jnp.dot(h, w2_ref[...], precision=hi, preferred_element_type=F32)
                                 + b2_ref[...]))
    h = jnp.dot(h, w3_ref[...], precision=hi, preferred_element_type=F32)
    decay = jnp.exp(-(t[:, 0:1]) * dl_ref[...])
    col = lax.broadcasted_iota(jnp.int32, h.shape, 1)
    first = (pos[:, 0:1] == 0.0) & (col >= width)
    o_ref[...] = jnp.where(first, 0.0, h * decay).astype(BF16)


def _hyena_filters(length, w1, b1, freq, w2, b2, w3):
    emb, fo = w1.shape
    width = w3.shape[1] // 2
    emb_bands = (emb - 1) // 2
    tl = _tile(length, TM_DFT)
    bands = jnp.linspace(1e-4, emb_bands - 1, emb_bands, dtype=F32)
    bands_row = jnp.zeros((1, V7X_LANES), F32).at[0, 1:1 + emb_bands].set(bands)
    bands_row = bands_row.at[0, 1 + emb_bands:1 + 2 * emb_bands].set(bands)
    w1p = jnp.zeros((V7X_LANES, fo), F32).at[:emb].set(w1)
    deltas = jnp.abs(jnp.linspace(math.log(HY_TARGET) / HY_SLOW, math.log(HY_TARGET) / HY_FAST, width, dtype=F32))
    dl = jnp.concatenate([deltas, deltas]).reshape(1, 2 * width)
    full = lambda shape: pl.BlockSpec(shape, lambda i: (0,) * len(shape))
    blocks = _nbytes((tl, 2 * width), BF16) + _nbytes((fo + 2, 2 * width), F32) + _nbytes((V7X_LANES + fo, fo), F32)
    return pl.pallas_call(
        functools.partial(_filter_kernel, length=length, tl=tl, emb_bands=emb_bands, width=width),
        out_shape=jax.ShapeDtypeStruct((length, 2 * width), BF16),
        grid=(length // tl,),
        in_specs=[full((1, V7X_LANES)), full((V7X_LANES, fo)), full((1, fo)), full((2, fo)),
                  full((fo, fo)), full((1, fo)), full((fo, 2 * width)), full((1, 2 * width))],
        out_specs=pl.BlockSpec((tl, 2 * width), lambda i: (i, 0)),
        compiler_params=_params(("arbitrary",), blocks, 6 * _nbytes((tl, 2 * width), F32)),
        name="hyena_filters",
    )(bands_row, w1p, b1.reshape(1, fo), freq, w2, b2.reshape(1, fo), w3, dl)


def _spectrum_kernel(c_ref, s_ref, hf_ref, hb_ref, fr_ref, fi_ref, *, tm, norm):
    cm, sm, hf, hb = c_ref[...], s_ref[...], hf_ref[...], hb_ref[...]
    row = lax.broadcasted_iota(jnp.int32, fr_ref.shape, 0) + pl.program_id(0) * tm
    packed = row == 0
    wgt = jnp.where(packed, 0.5 * norm, norm)
    bi = _dot(sm, hb)
    fr_ref[...] = (_dot(cm, hf) + _dot(cm, hb)) * wgt
    fi_ref[...] = (_dot(sm, hf) + jnp.where(packed, bi, -bi)) * wgt


def _filter_spectrum(cosm, sinm, hcat):
    length = cosm.shape[0]
    width = hcat.shape[1] // 2
    tm, tn = _tile(length, TM_DFT), _tile(width, TN_DFT)
    nb = width // tn
    blocks = 2 * _nbytes((tm, length), BF16) + 2 * _nbytes((length, tn), BF16) + 2 * _nbytes((tm, tn), F32)
    out = jax.ShapeDtypeStruct((length, width), F32)
    ospec = pl.BlockSpec((tm, tn), lambda i, j: (i, j))
    return pl.pallas_call(
        functools.partial(_spectrum_kernel, tm=tm, norm=1.0 / length),
        out_shape=(out, out),
        grid=(length // tm, nb),
        in_specs=[pl.BlockSpec((tm, length), lambda i, j: (i, 0)),
                  pl.BlockSpec((tm, length), lambda i, j: (i, 0)),
                  pl.BlockSpec((length, tn), lambda i, j: (0, j)),
                  pl.BlockSpec((length, tn), lambda i, j: (0, nb + j))],
        out_specs=(ospec, ospec),
        compiler_params=_params(("arbitrary", "arbitrary"), blocks, 16 * _nbytes((tm, tn), F32)),
        name="filter_spectrum",
    )(cosm, sinm, hcat, hcat)


def _dft_fwd_kernel(c_ref, s_ref, z_ref, fr_ref, fi_ref, yr_ref, yi_ref, *, tm):
    z = z_ref[...]
    zr, zi = _dot(c_ref[...], z), _dot(s_ref[...], z)
    fr, fi = fr_ref[...], fi_ref[...]
    packed = (lax.broadcasted_iota(jnp.int32, zr.shape, 0) + pl.program_id(0) * tm) == 0
    yr_ref[...] = jnp.where(packed, zr * fr, zr * fr - zi * fi).astype(BF16)
    yi_ref[...] = jnp.where(packed, zi * fi, zr * fi + zi * fr).astype(BF16)


def _dft_forward(cosm, sinm, z, fr, fi):
    bsz, length, width = z.shape
    tm, tn = _tile(length, TM_DFT), _tile(width, TN_DFT_FWD)
    blocks = (2 * _nbytes((tm, length), BF16) + _nbytes((length, tn), BF16) + 2 * _nbytes((tm, tn), F32)
              + 2 * _nbytes((tm, tn), BF16))
    out = jax.ShapeDtypeStruct((bsz, length, width), BF16)
    ospec = pl.BlockSpec((None, tm, tn), lambda i, b, j: (b, i, j))
    return pl.pallas_call(
        functools.partial(_dft_fwd_kernel, tm=tm),
        out_shape=(out, out),
        grid=(length // tm, bsz, width // tn),
        in_specs=[pl.BlockSpec((tm, length), lambda i, b, j: (i, 0)),
                  pl.BlockSpec((tm, length), lambda i, b, j: (i, 0)),
                  pl.BlockSpec((None, length, tn), lambda i, b, j: (b, 0, j)),
                  pl.BlockSpec((tm, tn), lambda i, b, j: (i, j)),
                  pl.BlockSpec((tm, tn), lambda i, b, j: (i, j))],
        out_specs=(ospec, ospec),
        compiler_params=_params(("arbitrary",) * 3, blocks, 16 * _nbytes((tm, tn), F32)),
        name="dft_forward",
    )(cosm, sinm, z, fr, fi)


def _dft_inv_kernel(c_ref, st_ref, yr_ref, yi_ref, x0_ref, z_ref, b_ref, o_ref):
    y = _dot(c_ref[...], yr_ref[...]) + _dot(st_ref[...], yi_ref[...])
    y = y + z_ref[...].astype(F32) * b_ref[...]
    o_ref[...] = (x0_ref[...].astype(F32) * y).astype(BF16)


def _dft_inverse(cosm, sinm_t, yr, yi, x0, z, bias):
    bsz, length, width = z.shape
    tm, tn = _tile(length, TM_DFT), _tile(width, TN_DFT)
    blocks = (2 * _nbytes((tm, length), BF16) + 2 * _nbytes((length, tn), BF16)
              + 3 * _nbytes((tm, tn), BF16) + _nbytes((1, tn), F32))
    tile = pl.BlockSpec((None, tm, tn), lambda i, b, j: (b, i, j))
    panel = pl.BlockSpec((None, length, tn), lambda i, b, j: (b, 0, j))
    return pl.pallas_call(
        _dft_inv_kernel,
        out_shape=jax.ShapeDtypeStruct((bsz, length, width), BF16),
        grid=(length // tm, bsz, width // tn),
        in_specs=[pl.BlockSpec((tm, length), lambda i, b, j: (i, 0)),
                  pl.BlockSpec((tm, length), lambda i, b, j: (i, 0)),
                  panel, panel, tile, tile,
                  pl.BlockSpec((1, tn), lambda i, b, j: (0, j))],
        out_specs=tile,
        compiler_params=_params(("arbitrary",) * 3, blocks, 16 * _nbytes((tm, tn), F32)),
        name="dft_inverse",
    )(cosm, sinm_t, yr, yi, x0, z, bias.reshape(1, width))


def _merge_kernel(ya_ref, yb_ref, yc_ref, ga_ref, gb_ref, gc_ref, wa_ref, wb_ref, wc_ref, o_ref):
    ga = jax.nn.sigmoid(ga_ref[...].astype(F32))
    gb = jax.nn.sigmoid(gb_ref[...].astype(F32))
    gc = jax.nn.sigmoid(gc_ref[...].astype(F32))
    m = ga * _dot(ya_ref[...], wa_ref[...])
    m = m + gb * _dot(yb_ref[...], wb_ref[...])
    m = m + gc * _dot(yc_ref[...], wc_ref[...])
    o_ref[...] = m.astype(BF16)


def _merge(ya, yb, yc, proj, gate_col0, wa, wb, wc, layer):
    t = ya.shape[0]
    d = wa.shape[2]
    tm, tn = _tile(t, TM_MM), _tile(d, TN_HALF)
    nb = d // tn
    base = gate_col0 // tn
    ka, kb, kc = ya.shape[1], yb.shape[1], yc.shape[1]
    blocks = (_nbytes((tm, ka + kb + kc), BF16) + 3 * _nbytes((tm, tn), proj.dtype)
              + _nbytes((ka + kb + kc, tn), BF16) + _nbytes((tm, tn), BF16))

    def gate(g):
        return pl.BlockSpec((tm, tn), lambda i, j: (i, base + g * nb + j))

    def panel(k):
        return pl.BlockSpec((tm, k), lambda i, j: (i, 0))

    def wcol(k):
        return pl.BlockSpec((None, k, tn), lambda i, j: (layer, 0, j))

    return pl.pallas_call(
        _merge_kernel,
        out_shape=jax.ShapeDtypeStruct((t, d), BF16),
        grid=(t // tm, nb),
        in_specs=[panel(ka), panel(kb), panel(kc), gate(0), gate(1), gate(2), wcol(ka), wcol(kb), wcol(kc)],
        out_specs=pl.BlockSpec((tm, tn), lambda i, j: (i, j)),
        compiler_params=_params(("arbitrary", "arbitrary"), blocks, 16 * _nbytes((tm, tn), F32)),
        name="gated_merge",
    )(ya, yb, yc, proj, proj, proj, wa, wb, wc)


def _lane_partial_sum(v):
    return functools.reduce(jnp.add, [v[:, t * V7X_LANES:(t + 1) * V7X_LANES]
                                      for t in range(v.shape[1] // V7X_LANES)])


def _mm_epilogue_kernel(a_ref, b_ref, x_ref, gpost_ref, gate_ref, *rest, n_tiles, n_chunks, by_columns,
                        with_next):
    if with_next:
        gnext_ref, sc_ref, sh_ref, xo_ref, h_ref, acc_ref = rest
    else:
        xo_ref, acc_ref = rest
    i = pl.program_id(0)
    k = pl.program_id(1)
    _, n_panels, tm, pw = acc_ref.shape
    d = n_panels * pw
    rc = tm // n_chunks
    ew = _tile(pw, EPI_COL_PIECE)
    chunk = jnp.minimum(k, n_chunks - 1)

    @pl.when((i == 0) & (k == 0))
    def _():
        acc_ref[...] = jnp.zeros_like(acc_ref)

    def accumulate_stages(slot):
        def whole():
            acc_ref[slot, k] = _dot(a_ref[...], b_ref[...]).astype(acc_ref.dtype)

        def panel(c):
            part = _dot(a_ref[...], b_ref[:, c * pw:(c + 1) * pw])
            acc_ref[slot, c] = jnp.where(k == 0, part, acc_ref[slot, c] + part)

        return [whole] if by_columns else [functools.partial(panel, c) for c in range(n_panels)]

    def epilogue_stages(slot):
        rows = pl.ds(pl.multiple_of(chunk * rc, rc), rc)
        pieces = [(c, slice(e * ew, (e + 1) * ew), slice(c * pw + e * ew, c * pw + (e + 1) * ew))
                  for c in range(n_panels) for e in range(pw // ew)]
        rstd = {}

        def scale_of(ssq):
            return lax.rsqrt(jnp.sum(ssq, axis=-1, keepdims=True) / d + RMS_EPS)

        def product_norm():
            rstd["post"] = scale_of(functools.reduce(
                jnp.add, [_lane_partial_sum(jnp.square(acc_ref[slot, c, rows, ps].astype(F32)))
                          for c, ps, _ in pieces]))

        def residual():
            ssq = jnp.zeros((rc, V7X_LANES), F32)
            for c, ps, cs in pieces:
                y = acc_ref[slot, c, rows, ps].astype(F32) * rstd["post"]
                xn = x_ref[:, cs] + y * (gpost_ref[:, cs] * gate_ref[:, cs])
                xo_ref[:, cs] = xn
                ssq = ssq + _lane_partial_sum(jnp.square(xn))
            rstd["next"] = scale_of(ssq)

        def next_prenorm():
            for _, _, cs in pieces:
                hn = xo_ref[:, cs] * rstd["next"] * (gnext_ref[:, cs] * (1.0 + sc_ref[:, cs]))
                h_ref[:, cs] = (hn + sh_ref[:, cs]).astype(BF16)

        return [product_norm, residual] + ([next_prenorm] if with_next else [])

    def run(*stage_lists):
        for stage in itertools.chain(*stage_lists):
            stage()

    @pl.when(i == 0)
    def _():
        run(accumulate_stages(0))

    inner = (i > 0) & (i < n_tiles)

    @pl.when(inner & (i % 2 == 0))
    def _():
        run(epilogue_stages(1), accumulate_stages(0))

    @pl.when(inner & (i % 2 == 1))
    def _():
        run(epilogue_stages(0), accumulate_stages(1))

    @pl.when(i == n_tiles)
    def _():
        run(epilogue_stages((n_tiles - 1) % 2))


def _matmul_epilogue(a, b, layer, x2, gain_post, mod_l, gate_idx, rows_per_batch, nxt):
    t, kdim = a.shape
    d = b.shape[2]
    by_columns = kdim <= EPI_FULL_DEPTH_MAX
    acc_dtype = BF16 if by_columns else F32
    tm, tpb = _row_tiling(mod_l, t, rows_per_batch, TM_EPI_FULL_DEPTH if by_columns else TM_EPI)
    n_tiles = t // tm
    if by_columns:
        pw = _tile(d, EPI_COL_PIECE)
        steps = d // pw
        a_spec = pl.BlockSpec((tm, kdim), lambda i, k: (jnp.minimum(i, n_tiles - 1), 0))
        b_spec = pl.BlockSpec((None, kdim, pw),
                              lambda i, k: (layer, 0, jnp.where(i == n_tiles, steps - 1, k)))
        blocks = _nbytes((tm, kdim), BF16) + _nbytes((kdim, pw), BF16)
    else:
        fits = [w for w in EPI_K_TILES if kdim % w == 0]
        tk = next((w for w in fits if kdim // w >= EPI_ROW_CHUNKS), fits[-1])
        pw = _tile(d, EPI_COL_CHUNK)
        steps = kdim // tk
        a_spec = pl.BlockSpec((tm, tk), lambda i, k: (jnp.minimum(i, n_tiles - 1),
                                                      jnp.where(i == n_tiles, steps - 1, k)))
        b_spec = pl.BlockSpec((None, tk, d),
                              lambda i, k: (layer, jnp.where(i == n_tiles, steps - 1, k), 0))
        blocks = _nbytes((tm, tk), BF16) + _nbytes((tk, d), BF16)
    n_chunks = max(n for n in range(1, min(steps, EPI_ROW_CHUNKS) + 1) if tm % (n * 2 * V7X_SUBLANES) == 0)
    rc = tm // n_chunks

    def prev_tile(i):
        return jnp.maximum(i - 1, 0)

    def chunk_index(i, k):
        return jnp.where(i == 0, 0, (i - 1) * n_chunks + jnp.minimum(k, n_chunks - 1))

    def vec(which):
        return pl.BlockSpec((None, None, 1, d), lambda i, k: (prev_tile(i) // tpb, which, 0, 0))

    row = pl.BlockSpec((1, d), lambda i, k: (0, 0))
    chunk_rows = pl.BlockSpec((rc, d), lambda i, k: (chunk_index(i, k), 0))
    in_specs = [a_spec, b_spec, chunk_rows, row, vec(gate_idx)]
    args = [a, b, x2, gain_post.reshape(1, d), mod_l]
    blocks += 2 * _nbytes((rc, d), F32) + 5 * _nbytes((1, d), F32)
    if nxt is None:
        out_shape = jax.ShapeDtypeStruct((t, d), F32)
        out_specs = chunk_rows
    else:
        gain_next, mod_next, sc_idx, sh_idx = nxt
        in_specs += [row, vec(sc_idx), vec(sh_idx)]
        args += [gain_next.reshape(1, d), mod_next, mod_next]
        out_shape = (jax.ShapeDtypeStruct((t, d), F32), jax.ShapeDtypeStruct((t, d), BF16))
        out_specs = (chunk_rows, chunk_rows)
        blocks += _nbytes((rc, d), BF16)
    temps = (_nbytes((2, tm, d), acc_dtype) + 3 * _nbytes((tm, pw), F32)
             + 8 * _nbytes((rc, _tile(pw, EPI_COL_PIECE)), F32))
    return pl.pallas_call(
        functools.partial(_mm_epilogue_kernel, n_tiles=n_tiles, n_chunks=n_chunks, by_columns=by_columns,
                          with_next=nxt is not None),
        out_shape=out_shape,
        grid=(n_tiles + 1, steps),
        in_specs=in_specs,
        out_specs=out_specs,
        scratch_shapes=[pltpu.VMEM((2, d // pw, tm, pw), acc_dtype)],
        compiler_params=_params(("arbitrary", "arbitrary"), blocks, temps),
        name="matmul_norm_residual",
    )(*args)


def _ffn_up_kernel(h_ref, wg_ref, wu_ref, o_ref):
    h = h_ref[...]
    o_ref[...] = (jax.nn.silu(_dot(h, wg_ref[...])) * _dot(h, wu_ref[...])).astype(BF16)


def _ffn_up(h, wg, wu, layer):
    t, d = h.shape
    n = wg.shape[2]
    tm, tn = _tile(t, TM_MM), _tile(n, TN_HALF)
    blocks = _nbytes((tm, d), BF16) + 2 * _nbytes((d, tn), BF16) + _nbytes((tm, tn), BF16)
    return pl.pallas_call(
        _ffn_up_kernel,
        out_shape=jax.ShapeDtypeStruct((t, n), BF16),
        grid=(t // tm, n // tn),
        in_specs=[pl.BlockSpec((tm, d), lambda i, j: (i, 0)),
                  pl.BlockSpec((None, d, tn), lambda i, j: (layer, 0, j)),
                  pl.BlockSpec((None, d, tn), lambda i, j: (layer, 0, j))],
        out_specs=pl.BlockSpec((tm, tn), lambda i, j: (i, j)),
        compiler_params=_params(("arbitrary", "arbitrary"), blocks, 16 * _nbytes((tm, tn), F32)),
        name="ffn_up",
    )(h, wg, wu)


def _pad_cols(w, mult):
    pad = (-w.shape[-1]) % mult
    return jnp.pad(w, ((0, 0), (0, 0), (0, pad))) if pad else w


def _pad_rows(w, mult):
    pad = (-w.shape[-2]) % mult
    return jnp.pad(w, ((0, 0), (0, pad), (0, 0))) if pad else w


def kernel(x_prompt, x_sample, cache_k, cache_v, c, c_ctx, w_mod, b_mod, norm_gains, w_in, conv_a, w_up_a, na_rpb, w_up_b, conv_c, filt_w1, filt_b1, filt_freq, filt_w2, filt_b2, filt_w3, hyena_bias, w_up_c, w_out, w_ffn_gate, w_ffn_up, w_ffn_down):
    depth, d, _ = w_mod.shape
    n_heads, head_dim = cache_k.shape[3], cache_k.shape[4]
    sc_w = conv_a.shape[-1]
    na_w = n_heads * head_dim
    hy_w = hyena_bias.shape[-1]
    kh_full = (na_rpb.shape[2] + 1) // 2
    kw = (na_rpb.shape[3] + 1) // 2
    col_na = 3 * sc_w
    col_hy = col_na + 3 * na_w
    col_gate = col_hy + 3 * hy_w
    assert x_sample.shape[1] % GRID_W == 0 and x_sample.shape[1] // GRID_W >= kh_full
    assert 1 + c.shape[0] <= MOD_ROWS_PAD

    w_in_b = w_in.astype(BF16)
    w_up_a_b, w_up_b_b, w_up_c_b = w_up_a.astype(BF16), w_up_b.astype(BF16), w_up_c.astype(BF16)
    w_out_b = w_out.astype(BF16)
    w_gate_b = _pad_cols(w_ffn_gate.astype(BF16), FFN_PAD)
    w_upf_b = _pad_cols(w_ffn_up.astype(BF16), FFN_PAD)
    w_down_b = _pad_rows(w_ffn_down.astype(BF16), FFN_PAD)

    c_rows = jnp.zeros((MOD_ROWS_PAD, d), F32).at[0].set(c_ctx).at[1:1 + c.shape[0]].set(c)
    mod = _modulation(c_rows, w_mod, b_mod).reshape(depth, MOD_ROWS_PAD, N_MOD, 1, d)
    cache_k4 = cache_k.reshape(cache_k.shape[:3] + (na_w,))
    cache_v4 = cache_v.reshape(cache_v.shape[:3] + (na_w,))

    def run_group(x3, mod_g, latent):
        bsz, length, _ = x3.shape
        proj_dtype = BF16 if latent else F32
        cosm, sinm, sinm_t = _dft_matrices(length)
        x2 = x3.reshape(bsz * length, d)
        h = _prenorm(x2, norm_gains[0, 0], mod_g[0], length)
        ks, vs = [], []
        for l in range(depth):
            proj = _matmul(h, w_in_b, l, proj_dtype)
            proj3 = proj.reshape(bsz, length, proj.shape[1])
            y_sc = _short_conv(proj3, conv_a[l], 0)
            if latent:
                tiles = _nat_block_tiles(_nat_bias_strips(na_rpb[l], kh_full, kw), length // GRID_W)
                y_na = _neighbourhood_attention(proj3, col_na, cache_k4, cache_v4, l, tiles, kh_full,
                                                n_heads, head_dim)
            else:
                y_na, k_l, v_l = _context_attention(proj3, col_na, n_heads, head_dim)
                ks.append(k_l)
                vs.append(v_l)
            x0, z = _hyena_pre(proj3, conv_c[l], col_hy)
            hcat = _hyena_filters(length, filt_w1[l], filt_b1[l], filt_freq[l], filt_w2[l], filt_b2[l], filt_w3[l])
            fr, fi = _filter_spectrum(cosm, sinm, hcat)
            yr, yi = _dft_forward(cosm, sinm, z, fr, fi)
            y_hy = _dft_inverse(cosm, sinm_t, yr, yi, x0, z, hyena_bias[l])
            merged = _merge(y_sc.reshape(-1, sc_w), y_na.reshape(-1, na_w), y_hy.reshape(-1, hy_w),
                            proj, col_gate, w_up_a_b, w_up_b_b, w_up_c_b, l)
            x2, h2 = _matmul_epilogue(merged, w_out_b, l, x2, norm_gains[l, 1], mod_g[l], 2, length,
                                      (norm_gains[l, 2], mod_g[l], 4, 3))
            hidden = _ffn_up(h2, w_gate_b, w_upf_b, l)
            if l + 1 < depth:
                x2, h = _matmul_epilogue(hidden, w_down_b, l, x2, norm_gains[l, 3], mod_g[l], 5, length,
                                         (norm_gains[l + 1, 0], mod_g[l + 1], 1, 0))
            else:
                x2 = _matmul_epilogue(hidden, w_down_b, l, x2, norm_gains[l, 3], mod_g[l], 5, length, None)
        return x2.reshape(bsz, length, d), ks, vs

    y_prompt, ks, vs = run_group(x_prompt, mod[:, 0:1], latent=False)
    y_sample, _, _ = run_group(x_sample, mod[:, 1:1 + c.shape[0]], latent=True)
    return (y_prompt, y_sample, jnp.stack(ks, axis=1), jnp.stack(vs, axis=1))
```

```python
import functools
import itertools
import math

import numpy as np
import jax
import jax.numpy as jnp
from jax import lax
from jax.experimental import pallas as pl
from jax.experimental.pallas import tpu as pltpu

F32 = jnp.float32
BF16 = jnp.bfloat16

GRID_W = 64
HY_FAST = 0.3
HY_SLOW = 1.5
HY_TARGET = 1e-2
RMS_EPS = 1e-6
N_MOD = 6

V7X_VMEM_BYTES = 64 * 1024 * 1024
V7X_VMEM_RESERVED_BYTES = 6 * 1024 * 1024
V7X_LANES = 128
V7X_SUBLANES = 8
MOD_ROWS_PAD = 16

TM_MM = 1024
TN_MM = 1024
TN_HALF = 512
TM_EPI = 512
TM_EPI_FULL_DEPTH = 1024
EPI_K_TILES = (1024, 512, 256, 128)
EPI_FULL_DEPTH_MAX = 4096
EPI_ROW_CHUNKS = 8
EPI_COL_CHUNK = 1024
EPI_COL_PIECE = 512
TM_DFT = 512
TN_DFT = 512
TN_DFT_FWD = 1024
CONV_TILE_ELEMS = 1024 * 1024
FFN_PAD = 1024
HEADS_PER_STEP_CTX = 8
HEADS_PER_STEP_NAT = 2
NAT_CTX_CHUNK = 512
NAT_BLOCK_ROWS = 4
NAT_UNROLL = 4
DFT_SPLIT = 64


def _tile(dim, pref):
    return pref if dim % pref == 0 else dim


def _nbytes(shape, dtype):
    return int(np.prod(shape)) * jnp.dtype(dtype).itemsize


def _params(semantics, block_bytes, temp_bytes=0):
    need = 2 * block_bytes + temp_bytes
    limit = min(V7X_VMEM_BYTES - V7X_VMEM_RESERVED_BYTES, max(need, 16 * 1024 * 1024))
    return pltpu.CompilerParams(dimension_semantics=semantics, vmem_limit_bytes=limit)


def _dot(a, b):
    return jnp.dot(a, b, preferred_element_type=F32)


def _dot_t(a, b):
    return lax.dot_general(a, b, (((1,), (1,)), ((), ())), preferred_element_type=F32)


def _rms(x, gain):
    return x * lax.rsqrt(jnp.mean(x * x, axis=-1, keepdims=True) + RMS_EPS) * gain


def _mod_kernel(c_ref, w_ref, b_ref, o_ref):
    a = jax.nn.silu(c_ref[...]).astype(BF16)
    o_ref[...] = _dot(a, w_ref[...].astype(BF16)) + b_ref[...]


def _modulation(c_rows, w_mod, b_mod):
    depth, d, n = w_mod.shape
    tn = _tile(n, TN_HALF)
    blocks = _nbytes((MOD_ROWS_PAD, d), F32) + _nbytes((d, tn), F32) + _nbytes((MOD_ROWS_PAD + 1, tn), F32)
    return pl.pallas_call(
        _mod_kernel,
        out_shape=jax.ShapeDtypeStruct((depth, MOD_ROWS_PAD, n), F32),
        grid=(depth, n // tn),
        in_specs=[pl.BlockSpec((MOD_ROWS_PAD, d), lambda l, j: (0, 0)),
                  pl.BlockSpec((None, d, tn), lambda l, j: (l, 0, j)),
                  pl.BlockSpec((None, 1, tn), lambda l, j: (l, 0, j))],
        out_specs=pl.BlockSpec((None, MOD_ROWS_PAD, tn), lambda l, j: (l, 0, j)),
        compiler_params=_params(("arbitrary", "arbitrary"), blocks, _nbytes((d, tn), BF16)),
        name="modulation",
    )(c_rows, w_mod, b_mod.reshape(depth, 1, n))


def _row_tiling(mod_l, total_rows, rows_per_batch, pref):
    shared = mod_l.shape[0] == 1
    tm = _tile(total_rows if shared else rows_per_batch, pref)
    return tm, (total_rows if shared else rows_per_batch) // tm


def _vec_spec(d, which, tiles_per_batch):
    return pl.BlockSpec((None, None, 1, d), lambda i: (i // tiles_per_batch, which, 0, 0))


def _prenorm_kernel(x_ref, g_ref, sc_ref, sh_ref, h_ref):
    y = _rms(x_ref[...], g_ref[...])
    h_ref[...] = (y * (1.0 + sc_ref[...]) + sh_ref[...]).astype(BF16)


def _prenorm(x2, gain, mod_l, rows_per_batch):
    t, d = x2.shape
    tm, tpb = _row_tiling(mod_l, t, rows_per_batch, TM_EPI)
    blocks = _nbytes((tm, d), F32) + _nbytes((tm, d), BF16) + 3 * _nbytes((1, d), F32)
    return pl.pallas_call(
        _prenorm_kernel,
        out_shape=jax.ShapeDtypeStruct((t, d), BF16),
        grid=(t // tm,),
        in_specs=[pl.BlockSpec((tm, d), lambda i: (i, 0)),
                  pl.BlockSpec((1, d), lambda i: (0, 0)),
                  _vec_spec(d, 1, tpb),
                  _vec_spec(d, 0, tpb)],
        out_specs=pl.BlockSpec((tm, d), lambda i: (i, 0)),
        compiler_params=_params(("arbitrary",), blocks, 2 * _nbytes((tm, d), F32)),
        name="prenorm",
    )(x2, gain.reshape(1, d), mod_l, mod_l)


def _mm_kernel(a_ref, b_ref, o_ref):
    o_ref[...] = _dot(a_ref[...], b_ref[...]).astype(o_ref.dtype)


def _matmul(a, b, layer, out_dtype):
    m, k = a.shape
    n = b.shape[2]
    tm, tn = _tile(m, TM_MM), _tile(n, TN_MM)
    blocks = _nbytes((tm, k), BF16) + _nbytes((k, tn), BF16) + _nbytes((tm, tn), out_dtype)
    return pl.pallas_call(
        _mm_kernel,
        out_shape=jax.ShapeDtypeStruct((m, n), out_dtype),
        grid=(m // tm, n // tn),
        in_specs=[pl.BlockSpec((tm, k), lambda i, j: (i, 0)),
                  pl.BlockSpec((None, k, tn), lambda i, j: (layer, 0, j))],
        out_specs=pl.BlockSpec((tm, tn), lambda i, j: (i, j)),
        compiler_params=_params(("arbitrary", "arbitrary"), blocks, 3 * _nbytes((tm, tn), F32)),
        name="in_proj",
    )(a, b)


def _dwconv3(s, w):
    n = s.shape[0]
    row = lax.broadcasted_iota(jnp.int32, s.shape, 0)
    prev = jnp.where(row == 0, 0.0, pltpu.roll(s, 1, 0))
    nxt = jnp.where(row == n - 1, 0.0, pltpu.roll(s, n - 1, 0))
    return prev * w[0:1, :] + s * w[1:2, :] + nxt * w[2:3, :]


def _sconv_kernel(b_ref, c_ref, x_ref, w_ref, o_ref):
    s = c_ref[...].astype(F32) * x_ref[...].astype(F32)
    o_ref[...] = (b_ref[...].astype(F32) * _dwconv3(s, w_ref[...])).astype(BF16)


def _short_conv(proj3, conv_w, col0):
    bsz, length, _ = proj3.shape
    width = conv_w.shape[1]
    tc = _tile(width, max(V7X_LANES, CONV_TILE_ELEMS // length))
    nb = width // tc
    base = col0 // tc
    blocks = 3 * _nbytes((length, tc), proj3.dtype) + _nbytes((3, tc), F32) + _nbytes((length, tc), BF16)

    def col(g):
        return pl.BlockSpec((None, length, tc), lambda b, j: (b, 0, base + g * nb + j))

    return pl.pallas_call(
        _sconv_kernel,
        out_shape=jax.ShapeDtypeStruct((bsz, length, width), BF16),
        grid=(bsz, nb),
        in_specs=[col(0), col(1), col(2), pl.BlockSpec((3, tc), lambda b, j: (0, j))],
        out_specs=pl.BlockSpec((None, length, tc), lambda b, j: (b, 0, j)),
        compiler_params=_params(("arbitrary", "arbitrary"), blocks, 6 * _nbytes((length, tc), F32)),
        name="short_conv",
    )(proj3, proj3, proj3, conv_w)


def _hyena_pre_kernel(p0_ref, p1_ref, p2_ref, w0_ref, w1_ref, w2_ref, x0_ref, z_ref):
    x0_ref[...] = _dwconv3(p0_ref[...].astype(F32), w0_ref[...]).astype(BF16)
    x1 = _dwconv3(p1_ref[...].astype(F32), w1_ref[...])
    v = _dwconv3(p2_ref[...].astype(F32), w2_ref[...])
    z_ref[...] = (x1 * v).astype(BF16)


def _hyena_pre(proj3, conv_w, col0):
    bsz, length, _ = proj3.shape
    width = conv_w.shape[1] // 3
    tc = _tile(width, max(V7X_LANES, CONV_TILE_ELEMS // length))
    nb = width // tc
    base = col0 // tc
    blocks = (3 * _nbytes((length, tc), proj3.dtype) + 3 * _nbytes((3, tc), F32)
              + 2 * _nbytes((length, tc), BF16))

    def col(g):
        return pl.BlockSpec((None, length, tc), lambda b, j: (b, 0, base + g * nb + j))

    def wcol(g):
        return pl.BlockSpec((3, tc), lambda b, j: (0, g * nb + j))

    out = jax.ShapeDtypeStruct((bsz, length, width), BF16)
    ospec = pl.BlockSpec((None, length, tc), lambda b, j: (b, 0, j))
    return pl.pallas_call(
        _hyena_pre_kernel,
        out_shape=(out, out),
        grid=(bsz, nb),
        in_specs=[col(0), col(1), col(2), wcol(0), wcol(1), wcol(2)],
        out_specs=(ospec, ospec),
        compiler_params=_params(("arbitrary", "arbitrary"), blocks, 8 * _nbytes((length, tc), F32)),
        name="hyena_pre",
    )(proj3, proj3, proj3, conv_w, conv_w, conv_w)


def _softmax_pv(parts):
    m = functools.reduce(jnp.maximum, [jnp.max(s, axis=-1, keepdims=True) for s, _ in parts])
    ps = [jnp.exp(s - m) for s, _ in parts]
    denom = functools.reduce(jnp.add, [jnp.sum(p, axis=-1, keepdims=True) for p in ps])
    acc = functools.reduce(jnp.add, [_dot(p.astype(BF16), v) for p, (_, v) in zip(ps, parts)])
    return acc / denom


def _ctx_attn_kernel(q_ref, k_ref, v_ref, o_ref, *cache_refs, heads, head_dim):
    scale = head_dim ** -0.5
    for h in range(heads):
        sl = slice(h * head_dim, (h + 1) * head_dim)
        q = q_ref[:, sl].astype(BF16)
        k = k_ref[:, sl].astype(BF16)
        v = v_ref[:, sl].astype(BF16)
        o_ref[:, sl] = _softmax_pv([(_dot_t(q, k) * scale, v)]).astype(BF16)
        if cache_refs:
            ko_ref, vo_ref = cache_refs
            ko_ref[:, h, :] = k_ref[:, sl]
            vo_ref[:, h, :] = v_ref[:, sl]


def _context_attention(proj3, col0, n_heads, head_dim):
    bsz, length, _ = proj3.shape
    hp = _heads_per_step(n_heads, head_dim, col0, HEADS_PER_STEP_CTX)
    bw = hp * head_dim
    nb = n_heads // hp
    base = col0 // bw
    na_w = n_heads * head_dim
    blocks = 3 * _nbytes((length, bw), proj3.dtype) + _nbytes((length, bw), BF16)
    y_shape = jax.ShapeDtypeStruct((bsz, length, na_w), BF16)
    y_spec = pl.BlockSpec((None, length, bw), lambda b, j: (b, 0, j))
    in_kernel_cache = hp % V7X_SUBLANES == 0 or hp == n_heads

    def col(g):
        return pl.BlockSpec((None, length, bw), lambda b, j: (b, 0, base + g * nb + j))

    if in_kernel_cache:
        kv_shape = jax.ShapeDtypeStruct((bsz, length, n_heads, head_dim), proj3.dtype)
        kv_spec = pl.BlockSpec((None, length, hp, head_dim), lambda b, j: (b, 0, j, 0))
        out_shape, out_specs = (y_shape, kv_shape, kv_shape), (y_spec, kv_spec, kv_spec)
        blocks += 2 * _nbytes((length, bw), proj3.dtype)
    else:
        out_shape, out_specs = y_shape, y_spec
    out = pl.pallas_call(
        functools.partial(_ctx_attn_kernel, heads=hp, head_dim=head_dim),
        out_shape=out_shape,
        grid=(bsz, nb),
        in_specs=[col(0), col(1), col(2)],
        out_specs=out_specs,
        compiler_params=_params(("arbitrary", "arbitrary"), blocks, 8 * _nbytes((length, length), F32)),
        name="context_attention",
    )(proj3, proj3, proj3)
    if in_kernel_cache:
        return out
    k = proj3[:, :, col0 + na_w:col0 + 2 * na_w].reshape(bsz, length, n_heads, head_dim)
    v = proj3[:, :, col0 + 2 * na_w:col0 + 3 * na_w].reshape(bsz, length, n_heads, head_dim)
    return out, k, v


def _heads_per_step(n_heads, head_dim, col0, cap):
    return max(h for h in range(1, cap + 1) if n_heads % h == 0 and col0 % (h * head_dim) == 0)


def _nat_kernel(q_ref, k_ref, v_ref, kc_ref, vc_ref, bias_ref, o_ref, mc_ref, lc_ref, oc_ref,
                *, rows, kh, heads, head_dim):
    scale = head_dim ** -0.5
    length = rows * GRID_W
    past = kc_ref.shape[0]
    cch = _tile(length, NAT_CTX_CHUNK)
    n_cch = length // cch
    c_unroll = NAT_UNROLL if n_cch % NAT_UNROLL == 0 else 1
    span = NAT_BLOCK_ROWS + kh
    n_blk = rows // NAT_BLOCK_ROWS
    b_unroll = NAT_UNROLL if n_blk % NAT_UNROLL == 0 else 1
    qn, kn = NAT_BLOCK_ROWS * GRID_W, span * GRID_W
    lanes = [slice(h * head_dim, (h + 1) * head_dim) for h in range(heads)]

    def block_diag(parts):
        zero = jnp.zeros_like(parts[0])
        return jnp.concatenate([jnp.concatenate([p if j == i else zero for j in range(heads)], axis=-1)
                                for i, p in enumerate(parts)], axis=0)

    kc = block_diag([kc_ref[:, ls].astype(BF16) for ls in lanes])
    vc = block_diag([vc_ref[:, ls].astype(BF16) for ls in lanes])

    def ctx_group(g, carry):
        rws = [pl.ds(pl.multiple_of((g * c_unroll + u) * cch, cch), cch) for u in range(c_unroll)]
        scores = [_dot_t(q_ref[rw, :], kc) * scale for rw in rws]
        probs = []
        for rw, s in zip(rws, scores):
            per_head = []
            for h in range(heads):
                sh = s[:, h * past:(h + 1) * past]
                m = jnp.max(sh, axis=-1, keepdims=True)
                p = jnp.exp(sh - m)
                mc_ref[h, rw, :] = m
                lc_ref[h, rw, :] = jnp.sum(p, axis=-1, keepdims=True)
                per_head.append(p.astype(BF16))
            probs.append(jnp.concatenate(per_head, axis=-1))
        for rw, p in zip(rws, probs):
            oc_ref[rw, :] = _dot(p, vc)
        return carry

    lax.fori_loop(0, n_cch // c_unroll, ctx_group, 0)

    def block_group(g, carry):
        qrows, wins, kinds = [], [], []
        for u in range(b_unroll):
            blk = g * b_unroll + u
            r0 = blk * NAT_BLOCK_ROWS
            start = jnp.clip(r0 - kh // 2, 0, rows - span)
            qrows.append(pl.ds(pl.multiple_of(r0 * GRID_W, qn), qn))
            wins.append(pl.ds(pl.multiple_of(start * GRID_W, GRID_W), kn))
            kinds.append(jnp.where(blk == 0, 0, jnp.where(blk == n_blk - 1, 2, 1)))
        scores = [_dot_t(q_ref[qr, :], block_diag([k_ref[w, ls] for ls in lanes])) * scale
                  for qr, w in zip(qrows, wins)]
        probs, wcs, denoms = [], [], []
        for qr, s, kd in zip(qrows, scores, kinds):
            per_head, wc_h, denom_h = [], [], []
            for h in range(heads):
                sh = s[:, h * kn:(h + 1) * kn] + bias_ref[h, kd]
                mc = mc_ref[h, qr, :]
                m = jnp.maximum(jnp.max(sh, axis=-1, keepdims=True), mc)
                p = jnp.exp(sh - m)
                wc = jnp.exp(mc - m)
                per_head.append(p.astype(BF16))
                wc_h.append(wc)
                denom_h.append(jnp.sum(p, axis=-1, keepdims=True) + lc_ref[h, qr, :] * wc)
            probs.append(jnp.concatenate(per_head, axis=-1))
            wcs.append(wc_h)
            denoms.append(denom_h)
        accs = [_dot(p, block_diag([v_ref[w, ls] for ls in lanes])) for p, w in zip(probs, wins)]
        for qr, acc, wc_h, denom_h in zip(qrows, accs, wcs, denoms):
            for h, ls in enumerate(lanes):
                o_ref[qr, ls] = ((acc[:, ls] + oc_ref[qr, ls] * wc_h[h]) / denom_h[h]).astype(BF16)
        return carry

    lax.fori_loop(0, n_blk // b_unroll, block_group, 0)


def _nat_bias_strips(rpb, kh, kw):
    n_heads = rpb.shape[0]
    qc = np.arange(GRID_W)[:, None]
    kc = np.arange(GRID_W)[None, :]
    cstart = np.clip(qc - kw // 2, 0, GRID_W - kw)
    ok = (kc >= cstart) & (kc < cstart + kw)
    padded = jnp.pad(rpb, ((0, 0), (0, 0), (GRID_W - kw, GRID_W - kw)))
    toep = jnp.stack([padded[:, :, GRID_W - 1 - q:2 * GRID_W - 1 - q] for q in range(GRID_W)], axis=2)
    toep = jnp.where(ok[None, None], toep, -jnp.inf)
    return jnp.stack([jnp.transpose(toep[:, o:o + kh], (0, 2, 1, 3)).reshape(n_heads, GRID_W, kh * GRID_W)
                      for o in range(kh)], axis=1)


def _nat_block_tiles(strips, rows):
    n_heads, kh = strips.shape[:2]
    span = NAT_BLOCK_ROWS + kh
    n_blk = rows // NAT_BLOCK_ROWS
    assert rows % NAT_BLOCK_ROWS == 0 and rows >= span and NAT_BLOCK_ROWS >= kh // 2

    def ninf(width):
        return jnp.full((n_heads, GRID_W, width * GRID_W), -jnp.inf, F32)

    kinds = []
    for blk in (0, min(1, n_blk - 1), n_blk - 1):
        r0 = blk * NAT_BLOCK_ROWS
        start = min(max(r0 - kh // 2, 0), rows - span)
        slabs = []
        for r in range(r0, r0 + NAT_BLOCK_ROWS):
            rs = min(max(r - kh // 2, 0), rows - kh)
            lead = rs - start
            slabs.append(jnp.concatenate([ninf(lead), strips[:, rs - r + kh - 1], ninf(span - kh - lead)], axis=-1))
        kinds.append(jnp.concatenate(slabs, axis=1))
    return jnp.stack(kinds, axis=1)


def _neighbourhood_attention(proj3, col0, cache_k4, cache_v4, layer, tiles, kh, n_heads, head_dim):
    bsz, length, _ = proj3.shape
    rows = length // GRID_W
    past = cache_k4.shape[2]
    hp = _heads_per_step(n_heads, head_dim, col0, HEADS_PER_STEP_NAT)
    bw = hp * head_dim
    nb = n_heads // hp
    base = col0 // bw
    blocks = (3 * _nbytes((length, bw), proj3.dtype) + 2 * _nbytes((past, bw), F32)
              + _nbytes((hp,) + tiles.shape[1:], F32) + _nbytes((length, bw), BF16))
    scratch = [pltpu.VMEM((hp, length, 1), F32), pltpu.VMEM((hp, length, 1), F32), pltpu.VMEM((length, bw), F32)]
    temps = (2 * hp * _nbytes((length, V7X_LANES), F32) + _nbytes((length, bw), F32)
             + 6 * hp * NAT_UNROLL * _nbytes((_tile(length, NAT_CTX_CHUNK), past), F32)
             + 6 * hp * NAT_UNROLL * _nbytes(tiles.shape[2:], F32))

    def col(g):
        return pl.BlockSpec((None, length, bw), lambda b, j: (b, 0, base + g * nb + j))

    cache_spec = pl.BlockSpec((None, None, past, bw), lambda b, j: (b, layer, 0, j))
    return pl.pallas_call(
        functools.partial(_nat_kernel, rows=rows, kh=kh, heads=hp, head_dim=head_dim),
        out_shape=jax.ShapeDtypeStruct((bsz, length, n_heads * head_dim), BF16),
        grid=(bsz, nb),
        in_specs=[col(0), col(1), col(2), cache_spec, cache_spec,
                  pl.BlockSpec((hp,) + tiles.shape[1:], lambda b, j: (j, 0, 0, 0))],
        out_specs=pl.BlockSpec((None, length, bw), lambda b, j: (b, 0, j)),
        scratch_shapes=scratch,
        compiler_params=_params(("arbitrary", "arbitrary"), blocks, temps),
        name="neighbourhood_attention",
    )(proj3, proj3, proj3, cache_k4, cache_v4, tiles)


def _dft_matrices(length):
    n = 2 * length
    split = min(DFT_SPLIT, length)
    s = np.arange(length, dtype=np.int64)
    pa = ((np.arange(length // split, dtype=np.int64)[:, None] * split * s[None, :]) % n).astype(np.int32)
    pb = ((np.arange(split, dtype=np.int64)[:, None] * s[None, :]) % n).astype(np.int32)
    ta = jnp.asarray(pa).astype(F32) * (2.0 * math.pi / n)
    tb = jnp.asarray(pb).astype(F32) * (2.0 * math.pi / n)
    ca, sa, cb, sb = jnp.cos(ta), jnp.sin(ta), jnp.cos(tb), jnp.sin(tb)
    cosm = (ca[:, None, :] * cb[None, :, :] - sa[:, None, :] * sb[None, :, :]).reshape(length, length)
    msin = -(sa[:, None, :] * cb[None, :, :] + ca[:, None, :] * sb[None, :, :]).reshape(length, length)
    k_idx = lax.broadcasted_iota(jnp.int32, (length, length), 0)
    s_idx = lax.broadcasted_iota(jnp.int32, (length, length), 1)
    sinm = jnp.where(k_idx == 0, jnp.where(s_idx % 2 == 0, 1.0, -1.0), msin)
    sinm_t = jnp.where(s_idx == 0, jnp.where(k_idx % 2 == 0, 1.0, -1.0), msin)
    return cosm.astype(BF16), sinm.astype(BF16), sinm_t.astype(BF16)


def _filter_kernel(bands_ref, w1_ref, b1_ref, f_ref, w2_ref, b2_ref, w3_ref, dl_ref, o_ref,
                   *, length, tl, emb_bands, width):
    hi = lax.Precision.HIGHEST
    pos = (lax.broadcasted_iota(jnp.int32, (tl, V7X_LANES), 0) + pl.program_id(0) * tl).astype(F32)
    lane = lax.broadcasted_iota(jnp.int32, (tl, V7X_LANES), 1)
    t = pos * (1.0 / (length - 1))
    ang = ((2.0 * math.pi / length) * pos) * bands_ref[...]
    feat = jnp.where(lane == 0, t,
                     jnp.where(lane <= emb_bands, jnp.cos(ang),
                               jnp.where(lane <= 2 * emb_bands, -jnp.sin(ang), 0.0)))
    h = jnp.sin(f_ref[0:1, :] * (jnp.dot(feat, w1_ref[...], precision=hi, preferred_element_type=F32)
                                 + b1_ref[...]))
    h = jnp.sin(f_ref[1:2, :] * (jnp.dot(h, w2_ref[...], precision=hi, preferred_element_type=F32)
                                 + b2_ref[...]))
    h = jnp.dot(h, w3_ref[...], precision=hi, preferred_element_type=F32)
    decay = jnp.exp(-(t[:, 0:1]) * dl_ref[...])
    col = lax.broadcasted_iota(jnp.int32, h.shape, 1)
    first = (pos[:, 0:1] == 0.0) & (col >= width)
    o_ref[...] = jnp.where(first, 0.0, h * decay).astype(BF16)


def _hyena_filters(length, w1, b1, freq, w2, b2, w3):
    emb, fo = w1.shape
    width = w3.shape[1] // 2
    emb_bands = (emb - 1) // 2
    tl = _tile(length, TM_DFT)
    bands = jnp.linspace(1e-4, emb_bands - 1, emb_bands, dtype=F32)
    bands_row = jnp.zeros((1, V7X_LANES), F32).at[0, 1:1 + emb_bands].set(bands)
    bands_row = bands_row.at[0, 1 + emb_bands:1 + 2 * emb_bands].set(bands)
    w1p = jnp.zeros((V7X_LANES, fo), F32).at[:emb].set(w1)
    deltas = jnp.abs(jnp.linspace(math.log(HY_TARGET) / HY_SLOW, math.log(HY_TARGET) / HY_FAST, width, dtype=F32))
    dl = jnp.concatenate([deltas, deltas]).reshape(1, 2 * width)
    full = lambda shape: pl.BlockSpec(shape, lambda i: (0,) * len(shape))
    blocks = _nbytes((tl, 2 * width), BF16) + _nbytes((fo + 2, 2 * width), F32) + _nbytes((V7X_LANES + fo, fo), F32)
    return pl.pallas_call(
        functools.partial(_filter_kernel, length=length, tl=tl, emb_bands=emb_bands, width=width),
        out_shape=jax.ShapeDtypeStruct((length, 2 * width), BF16),
        grid=(length // tl,),
        in_specs=[full((1, V7X_LANES)), full((V7X_LANES, fo)), full((1, fo)), full((2, fo)),
                  full((fo, fo)), full((1, fo)), full((fo, 2 * width)), full((1, 2 * width))],
        out_specs=pl.BlockSpec((tl, 2 * width), lambda i: (i, 0)),
        compiler_params=_params(("arbitrary",), blocks, 6 * _nbytes((tl, 2 * width), F32)),
        name="hyena_filters",
    )(bands_row, w1p, b1.reshape(1, fo), freq, w2, b2.reshape(1, fo), w3, dl)


def _spectrum_kernel(c_ref, s_ref, hf_ref, hb_ref, fr_ref, fi_ref, *, tm, norm):
    cm, sm, hf, hb = c_ref[...], s_ref[...], hf_ref[...], hb_ref[...]
    row = lax.broadcasted_iota(jnp.int32, fr_ref.shape, 0) + pl.program_id(0) * tm
    packed = row == 0
    wgt = jnp.where(packed, 0.5 * norm, norm)
    bi = _dot(sm, hb)
    fr_ref[...] = (_dot(cm, hf) + _dot(cm, hb)) * wgt
    fi_ref[...] = (_dot(sm, hf) + jnp.where(packed, bi, -bi)) * wgt


def _filter_spectrum(cosm, sinm, hcat):
    length = cosm.shape[0]
    width = hcat.shape[1] // 2
    tm, tn = _tile(length, TM_DFT), _tile(width, TN_DFT)
    nb = width // tn
    blocks = 2 * _nbytes((tm, length), BF16) + 2 * _nbytes((length, tn), BF16) + 2 * _nbytes((tm, tn), F32)
    out = jax.ShapeDtypeStruct((length, width), F32)
    ospec = pl.BlockSpec((tm, tn), lambda i, j: (i, j))
    return pl.pallas_call(
        functools.partial(_spectrum_kernel, tm=tm, norm=1.0 / length),
        out_shape=(out, out),
        grid=(length // tm, nb),
        in_specs=[pl.BlockSpec((tm, length), lambda i, j: (i, 0)),
                  pl.BlockSpec((tm, length), lambda i, j: (i, 0)),
                  pl.BlockSpec((length, tn), lambda i, j: (0, j)),
                  pl.BlockSpec((length, tn), lambda i, j: (0, nb + j))],
        out_specs=(ospec, ospec),
        compiler_params=_params(("arbitrary", "arbitrary"), blocks, 16 * _nbytes((tm, tn), F32)),
        name="filter_spectrum",
    )(cosm, sinm, hcat, hcat)


def _dft_fwd_kernel(c_ref, s_ref, z_ref, fr_ref, fi_ref, yr_ref, yi_ref, *, tm):
    z = z_ref[...]
    zr, zi = _dot(c_ref[...], z), _dot(s_ref[...], z)
    fr, fi = fr_ref[...], fi_ref[...]
    packed = (lax.broadcasted_iota(jnp.int32, zr.shape, 0) + pl.program_id(0) * tm) == 0
    yr_ref[...] = jnp.where(packed, zr * fr, zr * fr - zi * fi).astype(BF16)
    yi_ref[...] = jnp.where(packed, zi * fi, zr * fi + zi * fr).astype(BF16)


def _dft_forward(cosm, sinm, z, fr, fi):
    bsz, length, width = z.shape
    tm, tn = _tile(length, TM_DFT), _tile(width, TN_DFT_FWD)
    blocks = (2 * _nbytes((tm, length), BF16) + _nbytes((length, tn), BF16) + 2 * _nbytes((tm, tn), F32)
              + 2 * _nbytes((tm, tn), BF16))
    out = jax.ShapeDtypeStruct((bsz, length, width), BF16)
    ospec = pl.BlockSpec((None, tm, tn), lambda i, b, j: (b, i, j))
    return pl.pallas_call(
        functools.partial(_dft_fwd_kernel, tm=tm),
        out_shape=(out, out),
        grid=(length // tm, bsz, width // tn),
        in_specs=[pl.BlockSpec((tm, length), lambda i, b, j: (i, 0)),
                  pl.BlockSpec((tm, length), lambda i, b, j: (i, 0)),
                  pl.BlockSpec((None, length, tn), lambda i, b, j: (b, 0, j)),
                  pl.BlockSpec((tm, tn), lambda i, b, j: (i, j)),
                  pl.BlockSpec((tm, tn), lambda i, b, j: (i, j))],
        out_specs=(ospec, ospec),
        compiler_params=_params(("arbitrary",) * 3, blocks, 16 * _nbytes((tm, tn), F32)),
        name="dft_forward",
    )(cosm, sinm, z, fr, fi)


def _dft_inv_kernel(c_ref, st_ref, yr_ref, yi_ref, x0_ref, z_ref, b_ref, o_ref):
    y = _dot(c_ref[...], yr_ref[...]) + _dot(st_ref[...], yi_ref[...])
    y = y + z_ref[...].astype(F32) * b_ref[...]
    o_ref[...] = (x0_ref[...].astype(F32) * y).astype(BF16)


def _dft_inverse(cosm, sinm_t, yr, yi, x0, z, bias):
    bsz, length, width = z.shape
    tm, tn = _tile(length, TM_DFT), _tile(width, TN_DFT)
    blocks = (2 * _nbytes((tm, length), BF16) + 2 * _nbytes((length, tn), BF16)
              + 3 * _nbytes((tm, tn), BF16) + _nbytes((1, tn), F32))
    tile = pl.BlockSpec((None, tm, tn), lambda i, b, j: (b, i, j))
    panel = pl.BlockSpec((None, length, tn), lambda i, b, j: (b, 0, j))
    return pl.pallas_call(
        _dft_inv_kernel,
        out_shape=jax.ShapeDtypeStruct((bsz, length, width), BF16),
        grid=(length // tm, bsz, width // tn),
        in_specs=[pl.BlockSpec((tm, length), lambda i, b, j: (i, 0)),
                  pl.BlockSpec((tm, length), lambda i, b, j: (i, 0)),
                  panel, panel, tile, tile,
                  pl.BlockSpec((1, tn), lambda i, b, j: (0, j))],
        out_specs=tile,
        compiler_params=_params(("arbitrary",) * 3, blocks, 16 * _nbytes((tm, tn), F32)),
        name="dft_inverse",
    )(cosm, sinm_t, yr, yi, x0, z, bias.reshape(1, width))


def _merge_kernel(ya_ref, yb_ref, yc_ref, ga_ref, gb_ref, gc_ref, wa_ref, wb_ref, wc_ref, o_ref):
    ga = jax.nn.sigmoid(ga_ref[...].astype(F32))
    gb = jax.nn.sigmoid(gb_ref[...].astype(F32))
    gc = jax.nn.sigmoid(gc_ref[...].astype(F32))
    m = ga * _dot(ya_ref[...], wa_ref[...])
    m = m + gb * _dot(yb_ref[...], wb_ref[...])
    m = m + gc * _dot(yc_ref[...], wc_ref[...])
    o_ref[...] = m.astype(BF16)


def _merge(ya, yb, yc, proj, gate_col0, wa, wb, wc, layer):
    t = ya.shape[0]
    d = wa.shape[2]
    tm, tn = _tile(t, TM_MM), _tile(d, TN_HALF)
    nb = d // tn
    base = gate_col0 // tn
    ka, kb, kc = ya.shape[1], yb.shape[1], yc.shape[1]
    blocks = (_nbytes((tm, ka + kb + kc), BF16) + 3 * _nbytes((tm, tn), proj.dtype)
              + _nbytes((ka + kb + kc, tn), BF16) + _nbytes((tm, tn), BF16))

    def gate(g):
        return pl.BlockSpec((tm, tn), lambda i, j: (i, base + g * nb + j))

    def panel(k):
        return pl.BlockSpec((tm, k), lambda i, j: (i, 0))

    def wcol(k):
        return pl.BlockSpec((None, k, tn), lambda i, j: (layer, 0, j))

    return pl.pallas_call(
        _merge_kernel,
        out_shape=jax.ShapeDtypeStruct((t, d), BF16),
        grid=(t // tm, nb),
        in_specs=[panel(ka), panel(kb), panel(kc), gate(0), gate(1), gate(2), wcol(ka), wcol(kb), wcol(kc)],
        out_specs=pl.BlockSpec((tm, tn), lambda i, j: (i, j)),
        compiler_params=_params(("arbitrary", "arbitrary"), blocks, 16 * _nbytes((tm, tn), F32)),
        name="gated_merge",
    )(ya, yb, yc, proj, proj, proj, wa, wb, wc)


def _lane_partial_sum(v):
    return functools.reduce(jnp.add, [v[:, t * V7X_LANES:(t + 1) * V7X_LANES]
                                      for t in range(v.shape[1] // V7X_LANES)])


def _mm_epilogue_kernel(a_ref, b_ref, x_ref, gpost_ref, gate_ref, *rest, n_tiles, n_chunks, by_columns,
                        with_next):
    if with_next:
        gnext_ref, sc_ref, sh_ref, xo_ref, h_ref, acc_ref = rest
    else:
        xo_ref, acc_ref = rest
    i = pl.program_id(0)
    k = pl.program_id(1)
    _, n_panels, tm, pw = acc_ref.shape
    d = n_panels * pw
    rc = tm // n_chunks
    ew = _tile(pw, EPI_COL_PIECE)
    chunk = jnp.minimum(k, n_chunks - 1)

    @pl.when((i == 0) & (k == 0))
    def _():
        acc_ref[...] = jnp.zeros_like(acc_ref)

    def accumulate_stages(slot):
        def whole():
            acc_ref[slot, k] = _dot(a_ref[...], b_ref[...]).astype(acc_ref.dtype)

        def panel(c):
            part = _dot(a_ref[...], b_ref[:, c * pw:(c + 1) * pw])
            acc_ref[slot, c] = jnp.where(k == 0, part, acc_ref[slot, c] + part)

        return [whole] if by_columns else [functools.partial(panel, c) for c in range(n_panels)]

    def epilogue_stages(slot):
        rows = pl.ds(pl.multiple_of(chunk * rc, rc), rc)
        pieces = [(c, slice(e * ew, (e + 1) * ew), slice(c * pw + e * ew, c * pw + (e + 1) * ew))
                  for c in range(n_panels) for e in range(pw // ew)]
        rstd = {}

        def scale_of(ssq):
            return lax.rsqrt(jnp.sum(ssq, axis=-1, keepdims=True) / d + RMS_EPS)

        def product_norm():
            rstd["post"] = scale_of(functools.reduce(
                jnp.add, [_lane_partial_sum(jnp.square(acc_ref[slot, c, rows, ps].astype(F32)))
                          for c, ps, _ in pieces]))

        def residual():
            ssq = jnp.zeros((rc, V7X_LANES), F32)
            for c, ps, cs in pieces:
                y = acc_ref[slot, c, rows, ps].astype(F32) * rstd["post"]
                xn = x_ref[:, cs] + y * (gpost_ref[:, cs] * gate_ref[:, cs])
                xo_ref[:, cs] = xn
                ssq = ssq + _lane_partial_sum(jnp.square(xn))
            rstd["next"] = scale_of(ssq)

        def next_prenorm():
            for _, _, cs in pieces:
                hn = xo_ref[:, cs] * rstd["next"] * (gnext_ref[:, cs] * (1.0 + sc_ref[:, cs]))
                h_ref[:, cs] = (hn + sh_ref[:, cs]).astype(BF16)

        return [product_norm, residual] + ([next_prenorm] if with_next else [])

    def run(*stage_lists):
        for stage in itertools.chain(*stage_lists):
            stage()

    @pl.when(i == 0)
    def _():
        run(accumulate_stages(0))

    inner = (i > 0) & (i < n_tiles)

    @pl.when(inner & (i % 2 == 0))
    def _():
        run(epilogue_stages(1), accumulate_stages(0))

    @pl.when(inner & (i % 2 == 1))
    def _():
        run(epilogue_stages(0), accumulate_stages(1))

    @pl.when(i == n_tiles)
    def _():
        run(epilogue_stages((n_tiles - 1) % 2))


def _matmul_epilogue(a, b, layer, x2, gain_post, mod_l, gate_idx, rows_per_batch, nxt):
    t, kdim = a.shape
    d = b.shape[2]
    by_columns = kdim <= EPI_FULL_DEPTH_MAX
    acc_dtype = BF16 if by_columns else F32
    tm, tpb = _row_tiling(mod_l, t, rows_per_batch, TM_EPI_FULL_DEPTH if by_columns else TM_EPI)
    n_tiles = t // tm
    if by_columns:
        pw = _tile(d, EPI_COL_PIECE)
        steps = d // pw
        a_spec = pl.BlockSpec((tm, kdim), lambda i, k: (jnp.minimum(i, n_tiles - 1), 0))
        b_spec = pl.BlockSpec((None, kdim, pw),
                              lambda i, k: (layer, 0, jnp.where(i == n_tiles, steps - 1, k)))
        blocks = _nbytes((tm, kdim), BF16) + _nbytes((kdim, pw), BF16)
    else:
        fits = [w for w in EPI_K_TILES if kdim % w == 0]
        tk = next((w for w in fits if kdim // w >= EPI_ROW_CHUNKS), fits[-1])
        pw = _tile(d, EPI_COL_CHUNK)
        steps = kdim // tk
        a_spec = pl.BlockSpec((tm, tk), lambda i, k: (jnp.minimum(i, n_tiles - 1),
                                                      jnp.where(i == n_tiles, steps - 1, k)))
        b_spec = pl.BlockSpec((None, tk, d),
                              lambda i, k: (layer, jnp.where(i == n_tiles, steps - 1, k), 0))
        blocks = _nbytes((tm, tk), BF16) + _nbytes((tk, d), BF16)
    n_chunks = max(n for n in range(1, min(steps, EPI_ROW_CHUNKS) + 1) if tm % (n * 2 * V7X_SUBLANES) == 0)
    rc = tm // n_chunks

    def prev_tile(i):
        return jnp.maximum(i - 1, 0)

    def chunk_index(i, k):
        return jnp.where(i == 0, 0, (i - 1) * n_chunks + jnp.minimum(k, n_chunks - 1))

    def vec(which):
        return pl.BlockSpec((None, None, 1, d), lambda i, k: (prev_tile(i) // tpb, which, 0, 0))

    row = pl.BlockSpec((1, d), lambda i, k: (0, 0))
    chunk_rows = pl.BlockSpec((rc, d), lambda i, k: (chunk_index(i, k), 0))
    in_specs = [a_spec, b_spec, chunk_rows, row, vec(gate_idx)]
    args = [a, b, x2, gain_post.reshape(1, d), mod_l]
    blocks += 2 * _nbytes((rc, d), F32) + 5 * _nbytes((1, d), F32)
    if nxt is None:
        out_shape = jax.ShapeDtypeStruct((t, d), F32)
        out_specs = chunk_rows
    else:
        gain_next, mod_next, sc_idx, sh_idx = nxt
        in_specs += [row, vec(sc_idx), vec(sh_idx)]
        args += [gain_next.reshape(1, d), mod_next, mod_next]
        out_shape = (jax.ShapeDtypeStruct((t, d), F32), jax.ShapeDtypeStruct((t, d), BF16))
        out_specs = (chunk_rows, chunk_rows)
        blocks += _nbytes((rc, d), BF16)
    temps = (_nbytes((2, tm, d), acc_dtype) + 3 * _nbytes((tm, pw), F32)
             + 8 * _nbytes((rc, _tile(pw, EPI_COL_PIECE)), F32))
    return pl.pallas_call(
        functools.partial(_mm_epilogue_kernel, n_tiles=n_tiles, n_chunks=n_chunks, by_columns=by_columns,
                          with_next=nxt is not None),
        out_shape=out_shape,
        grid=(n_tiles + 1, steps),
        in_specs=in_specs,
        out_specs=out_specs,
        scratch_shapes=[pltpu.VMEM((2, d // pw, tm, pw), acc_dtype)],
        compiler_params=_params(("arbitrary", "arbitrary"), blocks, temps),
        name="matmul_norm_residual",
    )(*args)


def _ffn_up_kernel(h_ref, wg_ref, wu_ref, o_ref):
    h = h_ref[...]
    o_ref[...] = (jax.nn.silu(_dot(h, wg_ref[...])) * _dot(h, wu_ref[...])).astype(BF16)


def _ffn_up(h, wg, wu, layer):
    t, d = h.shape
    n = wg.shape[2]
    tm, tn = _tile(t, TM_MM), _tile(n, TN_HALF)
    blocks = _nbytes((tm, d), BF16) + 2 * _nbytes((d, tn), BF16) + _nbytes((tm, tn), BF16)
    return pl.pallas_call(
        _ffn_up_kernel,
        out_shape=jax.ShapeDtypeStruct((t, n), BF16),
        grid=(t // tm, n // tn),
        in_specs=[pl.BlockSpec((tm, d), lambda i, j: (i, 0)),
                  pl.BlockSpec((None, d, tn), lambda i, j: (layer, 0, j)),
                  pl.BlockSpec((None, d, tn), lambda i, j: (layer, 0, j))],
        out_specs=pl.BlockSpec((tm, tn), lambda i, j: (i, j)),
        compiler_params=_params(("arbitrary", "arbitrary"), blocks, 16 * _nbytes((tm, tn), F32)),
        name="ffn_up",
    )(h, wg, wu)


def _cast_pad(w, axis, mult):
    pad = (-w.shape[axis]) % mult
    if not pad:
        return w.astype(BF16)
    shape = list(w.shape)
    shape[axis] += pad
    return lax.dynamic_update_slice(jnp.zeros(shape, BF16), w.astype(BF16), (0,) * w.ndim)


def kernel(x_prompt, x_sample, cache_k, cache_v, c, c_ctx, w_mod, b_mod, norm_gains, w_in, conv_a, w_up_a, na_rpb, w_up_b, conv_c, filt_w1, filt_b1, filt_freq, filt_w2, filt_b2, filt_w3, hyena_bias, w_up_c, w_out, w_ffn_gate, w_ffn_up, w_ffn_down):
    depth, d, _ = w_mod.shape
    n_heads, head_dim = cache_k.shape[3], cache_k.shape[4]
    sc_w = conv_a.shape[-1]
    na_w = n_heads * head_dim
    hy_w = hyena_bias.shape[-1]
    kh_full = (na_rpb.shape[2] + 1) // 2
    kw = (na_rpb.shape[3] + 1) // 2
    col_na = 3 * sc_w
    col_hy = col_na + 3 * na_w
    col_gate = col_hy + 3 * hy_w
    assert x_sample.shape[1] % GRID_W == 0 and x_sample.shape[1] // GRID_W >= kh_full
    assert 1 + c.shape[0] <= MOD_ROWS_PAD

    w_in_b = w_in.astype(BF16)
    w_up_a_b, w_up_b_b, w_up_c_b = w_up_a.astype(BF16), w_up_b.astype(BF16), w_up_c.astype(BF16)
    w_out_b = w_out.astype(BF16)
    w_gate_b = _cast_pad(w_ffn_gate, 2, FFN_PAD)
    w_upf_b = _cast_pad(w_ffn_up, 2, FFN_PAD)
    w_down_b = _cast_pad(w_ffn_down, 1, FFN_PAD)

    c_rows = jnp.zeros((MOD_ROWS_PAD, d), F32).at[0].set(c_ctx).at[1:1 + c.shape[0]].set(c)
    mod = _modulation(c_rows, w_mod, b_mod).reshape(depth, MOD_ROWS_PAD, N_MOD, 1, d)
    cache_k4 = cache_k.reshape(cache_k.shape[:3] + (na_w,))
    cache_v4 = cache_v.reshape(cache_v.shape[:3] + (na_w,))

    def run_group(x3, mod_g, latent):
        bsz, length, _ = x3.shape
        proj_dtype = BF16 if latent else F32
        cosm, sinm, sinm_t = _dft_matrices(length)
        x2 = x3.reshape(bsz * length, d)
        h = _prenorm(x2, norm_gains[0, 0], mod_g[0], length)
        ks, vs = [], []
        for l in range(depth):
            proj = _matmul(h, w_in_b, l, proj_dtype)
            proj3 = proj.reshape(bsz, length, proj.shape[1])
            y_sc = _short_conv(proj3, conv_a[l], 0)
            if latent:
                tiles = _nat_block_tiles(_nat_bias_strips(na_rpb[l], kh_full, kw), length // GRID_W)
                y_na = _neighbourhood_attention(proj3, col_na, cache_k4, cache_v4, l, tiles, kh_full,
                                                n_heads, head_dim)
            else:
                y_na, k_l, v_l = _context_attention(proj3, col_na, n_heads, head_dim)
                ks.append(k_l)
                vs.append(v_l)
            x0, z = _hyena_pre(proj3, conv_c[l], col_hy)
            hcat = _hyena_filters(length, filt_w1[l], filt_b1[l], filt_freq[l], filt_w2[l], filt_b2[l], filt_w3[l])
            fr, fi = _filter_spectrum(cosm, sinm, hcat)
            yr, yi = _dft_forward(cosm, sinm, z, fr, fi)
            y_hy = _dft_inverse(cosm, sinm_t, yr, yi, x0, z, hyena_bias[l])
            merged = _merge(y_sc.reshape(-1, sc_w), y_na.reshape(-1, na_w), y_hy.reshape(-1, hy_w),
                            proj, col_gate, w_up_a_b, w_up_b_b, w_up_c_b, l)
            x2, h2 = _matmul_epilogue(merged, w_out_b, l, x2, norm_gains[l, 1], mod_g[l], 2, length,
                                      (norm_gains[l, 2], mod_g[l], 4, 3))
            hidden = _ffn_up(h2, w_gate_b, w_upf_b, l)
            if l + 1 < depth:
                x2, h = _matmul_epilogue(hidden, w_down_b, l, x2, norm_gains[l, 3], mod_g[l], 5, length,
                                         (norm_gains[l + 1, 0], mod_g[l + 1], 1, 0))
            else:
                x2 = _matmul_epilogue(hidden, w_down_b, l, x2, norm_gains[l, 3], mod_g[l], 5, length, None)
        return x2.reshape(bsz, length, d), ks, vs

    y_prompt, ks, vs = run_group(x_prompt, mod[:, 0:1], latent=False)
    y_sample, _, _ = run_group(x_sample, mod[:, 1:1 + c.shape[0]], latent=True)
    return (y_prompt, y_sample, jnp.stack(ks, axis=1), jnp.stack(vs, axis=1))
```
